```python
import math
import jax
import jax.numpy as jnp
from jax import lax
import numpy as np

D_MODEL = 2048
BATCH = 2
SEQ = 4096
DEPTH = 4
DEC_BATCH = 8
DEC_SEQ = 1
PAST_LEN = 16384
PAGE_SIZE = 128

D_HEAD = 128
W_A = D_MODEL // 2
W_B = D_MODEL // 4
W_C = D_MODEL // 4
H_A = W_A // D_HEAD
H_B = W_B // D_HEAD
MOBA_BLOCK = 256
MOBA_TOPK = 3
MOBA_Q_BLOCK = 32
SB_Q_BLOCK = 128
POOL_WINDOWS = (2, 4, 8, 16)
C_GROUPS = len(POOL_WINDOWS)
C_GROUP_W = W_C // C_GROUPS
POOL_BUF = max(POOL_WINDOWS) - 1
ROPE_THETA = 10000.0
LN_EPS = 1e-5
ALPHA = (2 * DEPTH) ** 0.25
DEEPNORM_BETA = (8 * DEPTH) ** -0.25
SPLIT_SIZES = (W_A,) * 4 + (W_B,) * 4 + (W_C,) * 2
SPLIT_POINTS = tuple(int(v) for v in np.cumsum(SPLIT_SIZES)[:-1])
IN_COLS = sum(SPLIT_SIZES)

kernel_name = 'hymba_moba_stickbreak_pool_step'


def layer_norm(x, g, b):
    xf = x.astype(jnp.float32)
    mu = jnp.mean(xf, axis=-1, keepdims=True)
    var = jnp.mean(jnp.square(xf - mu), axis=-1, keepdims=True)
    return ((xf - mu) * lax.rsqrt(var + LN_EPS) * g + b).astype(x.dtype)


def rotary(x, pos):
    d = x.shape[-1]
    half = d // 2
    inv = 1.0 / (ROPE_THETA ** (jnp.arange(half, dtype=jnp.float32) * (2.0 / d)))
    ang = pos.astype(jnp.float32)[:, None] * inv[None, :]
    cos = jnp.cos(ang)[:, None, :]
    sin = jnp.sin(ang)[:, None, :]
    xf = x.astype(jnp.float32)
    x1, x2 = xf[..., :half], xf[..., half:]
    return jnp.concatenate([x1 * cos - x2 * sin, x2 * cos + x1 * sin], axis=-1).astype(x.dtype)


def project(x, pos, w):
    b_sz, t, _ = x.shape
    qa, ka, va, ga, qb, kb, vb, gb, uc, gc = jnp.split(x @ w, SPLIT_POINTS, axis=-1)
    qa = rotary(qa.reshape(b_sz, t, H_A, D_HEAD), pos)
    ka = rotary(ka.reshape(b_sz, t, H_A, D_HEAD), pos)
    va = va.reshape(b_sz, t, H_A, D_HEAD)
    qb = qb.reshape(b_sz, t, H_B, D_HEAD)
    kb = kb.reshape(b_sz, t, H_B, D_HEAD)
    vb = vb.reshape(b_sz, t, H_B, D_HEAD)
    return qa, ka, va, ga, qb, kb, vb, gb, uc, gc


def to_blocks(t):
    b_sz, length, n_h, d = t.shape
    n_blk = -(-length // MOBA_BLOCK)
    t = jnp.pad(t, ((0, 0), (0, n_blk * MOBA_BLOCK - length), (0, 0), (0, 0)))
    return t.reshape(b_sz, n_blk, MOBA_BLOCK, n_h, d).transpose(0, 3, 1, 2, 4)


def moba_attend(q, q_pos, k_blk, v_blk, k_mean):
    b_sz, t_q, n_h, d = q.shape
    n_blk = k_blk.shape[2]
    n_sel = min(MOBA_TOPK, n_blk)
    own = q_pos // MOBA_BLOCK
    gate = jnp.einsum('bthd,bhnd->bthn', q.astype(jnp.float32), k_mean)
    fully_past = jnp.arange(n_blk)[None, None, None, :] < own[None, :, None, None]
    gate = jnp.where(fully_past, gate, -jnp.inf)
    _, sel = lax.top_k(gate, n_sel)
    own_b = jnp.broadcast_to(own[None, :, None, None], (b_sz, t_q, n_h, 1)).astype(sel.dtype)
    ids = jnp.concatenate([sel, own_b], axis=-1)
    bi = jnp.arange(b_sz)[:, None, None, None]
    hi = jnp.arange(n_h)[None, None, :, None]
    k_g = k_blk[bi, hi, ids]
    v_g = v_blk[bi, hi, ids]
    k_pos = ids[..., None] * MOBA_BLOCK + jnp.arange(MOBA_BLOCK)
    slot_ok = (jnp.arange(n_sel + 1) == n_sel)[None, None, None, :] | (ids < own[None, :, None, None])
    mask = slot_ok[..., None] & (k_pos <= q_pos[None, :, None, None, None])
    s = jnp.einsum('bthd,bthspd->bthsp', q, k_g, preferred_element_type=jnp.float32) * (d ** -0.5)
    s = jnp.where(mask, s, -jnp.inf)
    p = jax.nn.softmax(s.reshape(b_sz, t_q, n_h, -1), axis=-1).reshape(s.shape)
    return jnp.einsum('bthsp,bthspd->bthd', p.astype(v_g.dtype), v_g)


def stick_breaking(q, q_pos, k, v):
    d = q.shape[-1]
    z = jnp.einsum('bthd,bshd->bhts', q, k, preferred_element_type=jnp.float32) * (d ** -0.5)
    before = jnp.arange(k.shape[1])[None, :] < q_pos[:, None]
    log_keep = jnp.where(before, jax.nn.log_sigmoid(-z), 0.0)
    after = lax.cumsum(log_keep, axis=3, reverse=True) - log_keep
    weight = jnp.where(before, jnp.exp(jax.nn.log_sigmoid(z) + after), 0.0)
    return jnp.einsum('bhts,bshd->bthd', weight.astype(v.dtype), v)


def map_query_blocks(fn, q, pos, blk):
    b_sz, t, n_h, d = q.shape
    n = t // blk
    qc = q.reshape(b_sz, n, blk, n_h, d).swapaxes(0, 1)
    pc = pos.reshape(n, blk)
    out = lax.map(lambda a: fn(a[0], a[1]), (qc, pc))
    return out.swapaxes(0, 1).reshape(b_sz, t, n_h, d)


def pool_mix(u, prev, pos, w_pool_l, scale_l):
    b_sz, t, c = u.shape
    xpad = jnp.concatenate([prev.astype(u.dtype), u], axis=1)
    xf = xpad.astype(jnp.float32)
    cs = jnp.concatenate([jnp.zeros((b_sz, 1, c), jnp.float32), jnp.cumsum(xf, axis=1)], axis=1)
    uf = u.astype(jnp.float32)
    outs = []
    for g, w in enumerate(POOL_WINDOWS):
        sl = slice(g * C_GROUP_W, (g + 1) * C_GROUP_W)
        lo = POOL_BUF + 1 - w
        win_sum = cs[:, POOL_BUF + 1:, sl] - cs[:, lo:lo + t, sl]
        cnt = jnp.minimum(pos + 1, w).astype(jnp.float32)[None, :, None]
        outs.append(win_sum / cnt - uf[:, :, sl])
    r = jnp.stack(outs, axis=2)
    mixed = jnp.einsum('btgc,gcd->btgd', r, w_pool_l.astype(jnp.float32)).reshape(b_sz, t, c)
    mixed = mixed * scale_l.astype(jnp.float32)
    return mixed.astype(u.dtype), xpad[:, -POOL_BUF:]


def merge_and_norm(x, oa, ga, ob, gb, oc, gc, w_out_l, g, b):
    b_sz, t, _ = x.shape
    mixed = jnp.concatenate([
        oa.reshape(b_sz, t, W_A) * jax.nn.silu(ga),
        ob.reshape(b_sz, t, W_B) * jax.nn.silu(gb),
        oc * jax.nn.silu(gc)], axis=-1)
    return layer_norm(ALPHA * x + mixed @ w_out_l, g, b)


def setup_inputs(seed: int = 0) -> dict:
    key = jax.random.key(seed)
    ks = jax.random.split(key, 14)
    n_pages = PAST_LEN // PAGE_SIZE
    n_used = DEC_BATCH * n_pages
    n_pool = n_used + max(1, n_used // 4)
    f32 = jnp.float32
    x_prompt = jax.random.normal(ks[0], (BATCH, SEQ, D_MODEL), f32)
    x_sample = jax.random.normal(ks[1], (DEC_BATCH, DEC_SEQ, D_MODEL), f32)
    cache_k_a = jax.random.normal(ks[2], (DEPTH, n_pool, PAGE_SIZE, H_A, D_HEAD), f32)
    cache_v_a = jax.random.normal(ks[3], (DEPTH, n_pool, PAGE_SIZE, H_A, D_HEAD), f32)
    cache_k_b = jax.random.normal(ks[4], (DEPTH, n_pool, PAGE_SIZE, H_B, D_HEAD), f32)
    cache_v_b = jax.random.normal(ks[5], (DEPTH, n_pool, PAGE_SIZE, H_B, D_HEAD), f32)
    state_pool = jax.random.normal(ks[6], (DEPTH, DEC_BATCH, POOL_BUF, W_C), f32)
    page_table = jax.random.permutation(ks[7], n_pool)[:n_used].reshape(DEC_BATCH, n_pages).astype(jnp.int32)
    col_scale = jnp.concatenate([
        jnp.full((s,), DEEPNORM_BETA if i in (2, 6) else 1.0, f32) for i, s in enumerate(SPLIT_SIZES)])
    w_in = jax.random.normal(ks[8], (DEPTH, D_MODEL, IN_COLS), f32) * (D_MODEL ** -0.5) * col_scale
    w_pool = jax.random.normal(ks[9], (DEPTH, C_GROUPS, C_GROUP_W, C_GROUP_W), f32) * (C_GROUP_W ** -0.5)
    pool_scale = 1.0 + 0.02 * jax.random.normal(ks[10], (DEPTH, W_C), f32)
    w_out = jax.random.normal(ks[11], (DEPTH, D_MODEL, D_MODEL), f32) * (D_MODEL ** -0.5) * DEEPNORM_BETA
    ln_g = 1.0 + 0.02 * jax.random.normal(ks[12], (DEPTH, D_MODEL), f32)
    ln_b = 0.02 * jax.random.normal(ks[13], (DEPTH, D_MODEL), f32)
    return {'x_prompt': x_prompt, 'x_sample': x_sample,
            'cache_k_a': cache_k_a, 'cache_v_a': cache_v_a,
            'cache_k_b': cache_k_b, 'cache_v_b': cache_v_b,
            'state_pool': state_pool, 'page_table': page_table,
            'w_in': w_in, 'w_pool': w_pool, 'pool_scale': pool_scale,
            'w_out': w_out, 'ln_g': ln_g, 'ln_b': ln_b}


def reference(x_prompt, x_sample, cache_k_a, cache_v_a, cache_k_b, cache_v_b, state_pool,
              page_table, w_in, w_pool, pool_scale, w_out, ln_g, ln_b):
    n_seq, n_pages = page_table.shape
    past_len = n_pages * PAGE_SIZE
    pos_p = jnp.arange(x_prompt.shape[1], dtype=jnp.int32)
    pos_s = past_len + jnp.arange(x_sample.shape[1], dtype=jnp.int32)
    xp, xs = x_prompt, x_sample
    pka, pva, pkb, pvb, ppool = [], [], [], [], []
    ska, sva, skb, svb, spool = [], [], [], [], []
    for l in range(DEPTH):
        qa, ka, va, ga, qb, kb, vb, gb, uc, gc = project(xp, pos_p, w_in[l])
        k_blk = to_blocks(ka)
        v_blk = to_blocks(va)
        k_mean = jnp.mean(k_blk.astype(jnp.float32), axis=3)
        oa = map_query_blocks(lambda q, p: moba_attend(q, p, k_blk, v_blk, k_mean), qa, pos_p, MOBA_Q_BLOCK)
        ob = map_query_blocks(lambda q, p: stick_breaking(q, p, kb, vb), qb, pos_p, SB_Q_BLOCK)
        zeros_prev = jnp.zeros((xp.shape[0], POOL_BUF, W_C), uc.dtype)
        oc, buf_p = pool_mix(uc, zeros_prev, pos_p, w_pool[l], pool_scale[l])
        xp = merge_and_norm(xp, oa, ga, ob, gb, oc, gc, w_out[l], ln_g[l], ln_b[l])
        pka.append(ka)
        pva.append(va)
        pkb.append(kb)
        pvb.append(vb)
        ppool.append(buf_p)
        qa, ka, va, ga, qb, kb, vb, gb, uc, gc = project(xs, pos_s, w_in[l])
        ka_all = jnp.concatenate([cache_k_a[l][page_table].reshape(n_seq, past_len, H_A, D_HEAD), ka], axis=1)
        va_all = jnp.concatenate([cache_v_a[l][page_table].reshape(n_seq, past_len, H_A, D_HEAD), va], axis=1)
        kb_all = jnp.concatenate([cache_k_b[l][page_table].reshape(n_seq, past_len, H_B, D_HEAD), kb], axis=1)
        vb_all = jnp.concatenate([cache_v_b[l][page_table].reshape(n_seq, past_len, H_B, D_HEAD), vb], axis=1)
        k_blk_s = to_blocks(ka_all)
        v_blk_s = to_blocks(va_all)
        k_mean_s = jnp.mean(k_blk_s.astype(jnp.float32), axis=3)
        oa = moba_attend(qa, pos_s, k_blk_s, v_blk_s, k_mean_s)
        ob = stick_breaking(qb, pos_s, kb_all, vb_all)
        oc, buf_s = pool_mix(uc, state_pool[l], pos_s, w_pool[l], pool_scale[l])
        xs = merge_and_norm(xs, oa, ga, ob, gb, oc, gc, w_out[l], ln_g[l], ln_b[l])
        ska.append(ka)
        sva.append(va)
        skb.append(kb)
        svb.append(vb)
        spool.append(buf_s)
    return (xp, xs,
            jnp.stack(pka), jnp.stack(pva), jnp.stack(pkb), jnp.stack(pvb), jnp.stack(ppool),
            jnp.stack(ska), jnp.stack(sva), jnp.stack(skb), jnp.stack(svb), jnp.stack(spool))
```

```python
import functools
import math

import jax
import jax.numpy as jnp
from jax import lax
from jax.experimental import pallas as pl
from jax.experimental.pallas import tpu as pltpu

F32 = jnp.float32
BF16 = jnp.bfloat16
NEG_INF = float("-inf")

D_HEAD = 128
MOBA_BLOCK = 256
MOBA_TOPK = 3
POOL_WINDOWS = (2, 4, 8, 16)
POOL_BUF = max(POOL_WINDOWS) - 1
ROPE_THETA = 10000.0
LN_EPS = 1e-5

_MIB = 1024 * 1024


def _cparams(semantics, vmem_mib):
    return pltpu.CompilerParams(dimension_semantics=semantics,
                                vmem_limit_bytes=vmem_mib * _MIB)


_NT = (((1,), (1,)), ((), ()))


def _nt_dot(a, b, precision=None):
    return lax.dot_general(a, b, _NT, precision=precision, preferred_element_type=F32)


def _split3(x):
    hi = x.astype(BF16)
    r = x - hi.astype(F32)
    mid = r.astype(BF16)
    lo = (r - mid.astype(F32)).astype(BF16)
    return hi, mid, lo


def _log_sigmoid_pair(z):
    t = jnp.log1p(jnp.exp(-jnp.abs(z)))
    return jnp.minimum(z, 0.0) - t, jnp.minimum(-z, 0.0) - t


def _silu(g):
    return g * jax.nn.sigmoid(g)


def _proj_kernel(*refs, rotary, emit_f32, emit_bf16, emit_mean, n_heads_tile):
    it = iter(refs)
    x_ref = next(it)
    w_ref = next(it)
    cos_ref = next(it) if rotary else None
    sin_ref = next(it) if rotary else None
    of_ref = next(it) if emit_f32 else None
    ob_ref = next(it) if emit_bf16 else None
    om_ref = next(it) if emit_mean else None

    acc = jnp.dot(x_ref[...], w_ref[...], preferred_element_type=F32)
    if rotary:
        cos = cos_ref[...]
        sin = sin_ref[...]
        parts = []
        for j in range(n_heads_tile):
            xh = acc[:, j * D_HEAD:(j + 1) * D_HEAD]
            parts.append(xh * cos + pltpu.roll(xh, D_HEAD // 2, axis=1) * sin)
        acc = jnp.concatenate(parts, axis=1) if len(parts) > 1 else parts[0]
    if emit_f32:
        of_ref[...] = acc
    if emit_bf16:
        ob_ref[...] = acc.astype(BF16)
    if emit_mean:
        tm, tn = acc.shape
        g = tm // MOBA_BLOCK
        om_ref[...] = jnp.sum(acc.reshape(g, MOBA_BLOCK, tn), axis=1) * (1.0 / MOBA_BLOCK)


def _proj(x_bf, w_bf, layer, col0, ncols, *, seq_len, cos_t=None, sin_t=None,
          emit_f32=True, emit_bf16=False, emit_mean=False, tm=512, tn=512):
    m, k = x_bf.shape
    tn = min(tn, ncols)
    assert m % tm == 0 and ncols % tn == 0 and col0 % tn == 0 and seq_len % tm == 0
    assert tm % MOBA_BLOCK == 0
    rotary = cos_t is not None
    n_m, n_n = m // tm, ncols // tn
    nb0 = col0 // tn
    pos_tiles = seq_len // tm

    in_specs = [
        pl.BlockSpec((tm, k), lambda n, i: (i, 0)),
        pl.BlockSpec((None, k, tn), lambda n, i: (layer, 0, nb0 + n)),
    ]
    args = [x_bf, w_bf]
    if rotary:
        in_specs += [pl.BlockSpec((tm, D_HEAD), lambda n, i: (i % pos_tiles, 0))] * 2
        args += [cos_t, sin_t]
    out_shape, out_specs = [], []
    if emit_f32:
        out_shape.append(jax.ShapeDtypeStruct((m, ncols), F32))
        out_specs.append(pl.BlockSpec((tm, tn), lambda n, i: (i, n)))
    if emit_bf16:
        out_shape.append(jax.ShapeDtypeStruct((m, ncols), BF16))
        out_specs.append(pl.BlockSpec((tm, tn), lambda n, i: (i, n)))
    if emit_mean:
        g = tm // MOBA_BLOCK
        out_shape.append(jax.ShapeDtypeStruct((n_m, g, ncols), F32))
        out_specs.append(pl.BlockSpec((None, g, tn), lambda n, i: (i, 0, n)))
    kern = functools.partial(_proj_kernel, rotary=rotary, emit_f32=emit_f32,
                             emit_bf16=emit_bf16, emit_mean=emit_mean,
                             n_heads_tile=tn // D_HEAD)
    return pl.pallas_call(
        kern,
        grid=(n_n, n_m),
        in_specs=in_specs,
        out_specs=out_specs,
        out_shape=out_shape,
        compiler_params=_cparams(("arbitrary", "arbitrary"), 48),
    )(*args)


def _moba_kernel(q_ref, k_ref, v_ref, km_ref, g_ref, o_ref, *, scale):
    i = pl.program_id(2)
    blk = MOBA_BLOCK
    q = q_ref[...]
    qb = q.astype(BF16)
    km = km_ref[...]
    nb = km.shape[0]

    gate = _nt_dot(q, km, precision=lax.Precision.HIGHEST)
    col = lax.broadcasted_iota(jnp.int32, gate.shape, 1)
    past = col < i
    gate = jnp.where(past, gate, NEG_INF)
    cnt = jnp.zeros(gate.shape, jnp.int32)
    for j in range(nb):
        gj = gate[:, j:j + 1]
        beats = jnp.where(gj > gate, 1, jnp.where(gj == gate, jnp.where(col > j, 1, 0), 0))
        cnt = cnt + beats
    sel = jnp.where(past, jnp.where(cnt < MOBA_TOPK, 1.0, 0.0), 0.0)

    row = lax.broadcasted_iota(jnp.int32, (blk, blk), 0)
    colk = lax.broadcasted_iota(jnp.int32, (blk, blk), 1)

    own = pl.multiple_of(i * blk, blk)
    s = _nt_dot(qb, k_ref[pl.ds(own, blk), :]) * scale
    s = jnp.where(colk <= row, s, NEG_INF)
    m = jnp.max(s, axis=1, keepdims=True)
    p = jnp.exp(s - m)
    l = jnp.sum(p, axis=1, keepdims=True)
    acc = jnp.dot(p.astype(BF16), v_ref[pl.ds(own, blk), :], preferred_element_type=F32)

    def body(n, carry):
        m, l, acc = carry
        start = pl.multiple_of(n * blk, blk)
        s = _nt_dot(qb, k_ref[pl.ds(start, blk), :]) * scale
        seln = jnp.sum(jnp.where(col == n, sel, 0.0), axis=1, keepdims=True)
        s = jnp.where(seln > 0.0, s, NEG_INF)
        m_new = jnp.maximum(m, jnp.max(s, axis=1, keepdims=True))
        alpha = jnp.exp(m - m_new)
        p = jnp.exp(s - m_new)
        l = alpha * l + jnp.sum(p, axis=1, keepdims=True)
        acc = alpha * acc + jnp.dot(p.astype(BF16), v_ref[pl.ds(start, blk), :],
                                    preferred_element_type=F32)
        return m_new, l, acc

    m, l, acc = lax.fori_loop(0, i, body, (m, l, acc))
    o = acc / l
    o_ref[...] = (o * _silu(g_ref[...])).astype(BF16)


def _moba_prompt(qa, ka_bf, va_bf, kmean, ga):
    b, t, wa = qa.shape
    h = wa // D_HEAD
    nq = t // MOBA_BLOCK
    nb = kmean.shape[1]
    kern = functools.partial(_moba_kernel, scale=D_HEAD ** -0.5)
    tile = pl.BlockSpec((None, MOBA_BLOCK, D_HEAD), lambda bi, hi, i: (bi, i, hi))
    full = pl.BlockSpec((None, t, D_HEAD), lambda bi, hi, i: (bi, 0, hi))
    return pl.pallas_call(
        kern,
        grid=(b, h, nq),
        in_specs=[tile, full, full,
                  pl.BlockSpec((None, nb, D_HEAD), lambda bi, hi, i: (bi, 0, hi)),
                  tile],
        out_specs=tile,
        out_shape=jax.ShapeDtypeStruct((b, t, wa), BF16),
        compiler_params=_cparams(("arbitrary",) * 3, 32),
    )(qa, ka_bf, va_bf, kmean, ga)


def _sb_kernel(q_ref, k_ref, v_ref, tri_ref, g_ref, o_ref, *, scale):
    i = pl.program_id(2)
    blk = q_ref.shape[0]
    qb = q_ref[...]
    tri = tri_ref[...]
    row = lax.broadcasted_iota(jnp.int32, (blk, blk), 0)
    colk = lax.broadcasted_iota(jnp.int32, (blk, blk), 1)

    def block(start, mask, carry, acc):
        z = _nt_dot(qb, k_ref[pl.ds(start, blk), :]) * scale
        ls, lk = _log_sigmoid_pair(z)
        if mask is not None:
            lk = jnp.where(mask, lk, 0.0)
        hi, mid, lo = _split3(lk)
        inner = (jnp.dot(hi, tri, preferred_element_type=F32)
                 + jnp.dot(mid, tri, preferred_element_type=F32)
                 + jnp.dot(lo, tri, preferred_element_type=F32))
        w = jnp.exp(ls + (carry + inner))
        if mask is not None:
            w = jnp.where(mask, w, 0.0)
        acc = acc + jnp.dot(w.astype(BF16), v_ref[pl.ds(start, blk), :],
                            preferred_element_type=F32)
        carry = carry + (inner[:, 0:1] + lk[:, 0:1])
        return carry, acc

    own = pl.multiple_of(i * blk, blk)
    carry, acc = block(own, colk < row, jnp.zeros((blk, 1), F32),
                       jnp.zeros((blk, D_HEAD), F32))

    def body(j, c):
        start = pl.multiple_of((i - 1 - j) * blk, blk)
        return block(start, None, *c)

    carry, acc = lax.fori_loop(0, i, body, (carry, acc))
    o_ref[...] = (acc * _silu(g_ref[...])).astype(BF16)


def _sb_prompt(qb_bf, kb_bf, vb_bf, gb, blk=256):
    b, t, wb = qb_bf.shape
    h = wb // D_HEAD
    nq = t // blk
    r = jnp.arange(blk)
    tri = (r[:, None] > r[None, :]).astype(BF16)
    kern = functools.partial(_sb_kernel, scale=D_HEAD ** -0.5)
    tile = pl.BlockSpec((None, blk, D_HEAD), lambda bi, hi, i: (bi, i, hi))
    full = pl.BlockSpec((None, t, D_HEAD), lambda bi, hi, i: (bi, 0, hi))
    return pl.pallas_call(
        kern,
        grid=(b, h, nq),
        in_specs=[tile, full, full,
                  pl.BlockSpec((blk, blk), lambda bi, hi, i: (0, 0)),
                  tile],
        out_specs=tile,
        out_shape=jax.ShapeDtypeStruct((b, t, wb), BF16),
        compiler_params=_cparams(("arbitrary",) * 3, 32),
    )(qb_bf, kb_bf, vb_bf, tri, gb)


def _pool_kernel(u_ref, halo_ref, wp_ref, sc_ref, g_ref, o_ref, xp_ref, *, tq):
    i = pl.program_id(1)
    cg = D_HEAD
    halo = halo_ref[...]
    xp_ref[0:POOL_BUF + 1, :] = jnp.where(i > 0, halo, 0.0)
    xp_ref[POOL_BUF + 1:, :] = u_ref[...]
    pos = i * tq + lax.broadcasted_iota(jnp.int32, (tq, 1), 0)
    outs = []
    for gi, w in enumerate(POOL_WINDOWS):
        sl = slice(gi * cg, (gi + 1) * cg)
        x = xp_ref[POOL_BUF + 1:, sl]
        win = x
        for d in range(1, w):
            win = win + xp_ref[pl.ds(POOL_BUF + 1 - d, tq), sl]
        cnt = jnp.minimum(pos + 1, w).astype(F32)
        r = win / cnt - x
        outs.append(jnp.dot(r.astype(BF16), wp_ref[gi].astype(BF16),
                            preferred_element_type=F32))
    mixed = jnp.concatenate(outs, axis=1) * sc_ref[...]
    o_ref[...] = (mixed * _silu(g_ref[...])).astype(BF16)


def _pool_prompt(uc, gc, w_pool, pool_scale, layer, tq=512):
    b, t, wc = uc.shape
    hb = POOL_BUF + 1
    halo_per_tile = tq // hb
    kern = functools.partial(_pool_kernel, tq=tq)
    tile = pl.BlockSpec((None, tq, wc), lambda bi, i: (bi, i, 0))
    return pl.pallas_call(
        kern,
        grid=(b, t // tq),
        in_specs=[tile,
                  pl.BlockSpec((None, hb, wc),
                               lambda bi, i: (bi, jnp.maximum(i * halo_per_tile - 1, 0), 0)),
                  pl.BlockSpec((None,) + w_pool.shape[1:], lambda bi, i: (layer, 0, 0, 0)),
                  pl.BlockSpec((None, 1, wc), lambda bi, i: (layer, 0, 0)),
                  tile],
        out_specs=tile,
        out_shape=jax.ShapeDtypeStruct((b, t, wc), BF16),
        scratch_shapes=[pltpu.VMEM((tq + hb, wc), F32)],
        compiler_params=_cparams(("arbitrary", "arbitrary"), 32),
    )(uc, uc, w_pool, pool_scale.reshape(pool_scale.shape[0], 1, wc), gc)


def _layer_norm(y, g, b):
    mu = jnp.mean(y, axis=-1, keepdims=True)
    d = y - mu
    var = jnp.mean(d * d, axis=-1, keepdims=True)
    return d * lax.rsqrt(var + LN_EPS) * g + b


def _merge_kernel(ma_ref, mb_ref, mc_ref, wa_ref, wb_ref, wc_ref, x_ref, g_ref, b_ref,
                  of_ref, ob_ref, *, alpha):
    y = (alpha * x_ref[...]
         + jnp.dot(ma_ref[...], wa_ref[...], preferred_element_type=F32)
         + jnp.dot(mb_ref[...], wb_ref[...], preferred_element_type=F32)
         + jnp.dot(mc_ref[...], wc_ref[...], preferred_element_type=F32))
    out = _layer_norm(y, g_ref[...], b_ref[...])
    of_ref[...] = out
    ob_ref[...] = out.astype(BF16)


def _merge(mix_a, mix_b, mix_c, w_out_bf, x, ln_g, ln_b, layer, alpha, tm=256):
    m, d = x.shape
    wa, wb, wc = mix_a.shape[1], mix_b.shape[1], mix_c.shape[1]
    assert wb == wc and wa % wb == 0
    kern = functools.partial(_merge_kernel, alpha=alpha)
    vec = pl.BlockSpec((None, 1, d), lambda i: (layer, 0, 0))
    row = lambda w: pl.BlockSpec((tm, w), lambda i: (i, 0))
    return pl.pallas_call(
        kern,
        grid=(m // tm,),
        in_specs=[row(wa), row(wb), row(wc),
                  pl.BlockSpec((None, wa, d), lambda i: (layer, 0, 0)),
                  pl.BlockSpec((None, wb, d), lambda i: (layer, wa // wb, 0)),
                  pl.BlockSpec((None, wc, d), lambda i: (layer, wa // wb + 1, 0)),
                  row(d), vec, vec],
        out_specs=[row(d), row(d)],
        out_shape=[jax.ShapeDtypeStruct((m, d), F32), jax.ShapeDtypeStruct((m, d), BF16)],
        compiler_params=_cparams(("arbitrary",), 48),
    )(mix_a, mix_b, mix_c, w_out_bf, w_out_bf, w_out_bf, x,
      ln_g.reshape(ln_g.shape[0], 1, d), ln_b.reshape(ln_b.shape[0], 1, d))


def _proj_s_kernel(x_ref, w_ref, cos_ref, sin_ref, o_ref, *, rot_tiles):
    n = pl.program_id(0)
    acc = jnp.dot(x_ref[...], w_ref[...], preferred_element_type=F32)
    cos = cos_ref[...]
    sin = sin_ref[...]
    parts = []
    for j in range(acc.shape[1] // D_HEAD):
        xh = acc[:, j * D_HEAD:(j + 1) * D_HEAD]
        parts.append(xh * cos + pltpu.roll(xh, D_HEAD // 2, axis=1) * sin)
    rot = jnp.concatenate(parts, axis=1)
    o_ref[...] = jnp.where(n < rot_tiles, rot, acc)


def _proj_sample(xs_bf, w_bf, layer, cos_row, sin_row, rot_cols, tn=512):
    r, k = xs_bf.shape
    ncols = w_bf.shape[2]
    kern = functools.partial(_proj_s_kernel, rot_tiles=rot_cols // tn)
    return pl.pallas_call(
        kern,
        grid=(ncols // tn,),
        in_specs=[pl.BlockSpec((r, k), lambda n: (0, 0)),
                  pl.BlockSpec((None, k, tn), lambda n: (layer, 0, n)),
                  pl.BlockSpec((r, D_HEAD), lambda n: (0, 0)),
                  pl.BlockSpec((r, D_HEAD), lambda n: (0, 0))],
        out_specs=pl.BlockSpec((r, tn), lambda n: (0, n)),
        out_shape=jax.ShapeDtypeStruct((r, ncols), F32),
        compiler_params=_cparams(("arbitrary",), 32),
    )(xs_bf, w_bf, cos_row, sin_row)


def _stream_kernel(pt_ref, *refs, pps, n_steps, h_a, h_b, scale):
    del pt_ref
    ka_refs = refs[0:pps]
    kb_refs = refs[pps:2 * pps]
    vb_refs = refs[2 * pps:3 * pps]
    qa_ref, qb_ref, tri_ref = refs[3 * pps:3 * pps + 3]
    ids_ref, ob_ref = refs[3 * pps + 3:3 * pps + 5]
    ksum_ref, carry_ref, acc_ref = refs[3 * pps + 5:]
    s = pl.program_id(1)
    page = ka_refs[0].shape[0]
    wa = h_a * D_HEAD
    wb = h_b * D_HEAD
    nblk = ksum_ref.shape[0]

    @pl.when(s == 0)
    def _():
        ksum_ref[...] = jnp.zeros_like(ksum_ref)
        carry_ref[...] = jnp.zeros_like(carry_ref)
        acc_ref[...] = jnp.zeros_like(acc_ref)

    def head_rows(q_flat, width):
        r = lax.broadcasted_iota(jnp.int32, (D_HEAD, width), 0)
        c = lax.broadcasted_iota(jnp.int32, (D_HEAD, width), 1)
        return jnp.where(r == jnp.right_shift(c, 7), jnp.broadcast_to(q_flat, (D_HEAD, width)), 0.0)

    rtb = head_rows(qb_ref[...], wb).astype(BF16)
    tri = tri_ref[...]
    blk_row = lax.broadcasted_iota(jnp.int32, (nblk, wa), 0)

    last_page = n_steps * pps - 1
    for j in range(pps):
        p_idx = last_page - (s * pps + j)
        colsum = jnp.sum(ka_refs[j][...], axis=0, keepdims=True)
        ksum_ref[...] += jnp.where(blk_row == p_idx // 2,
                                   jnp.broadcast_to(colsum, (nblk, wa)), 0.0)
        kb = kb_refs[j][...].astype(BF16)
        z = _nt_dot(kb, rtb) * scale
        ls, lk = _log_sigmoid_pair(z)
        hi, mid, lo = _split3(lk)
        inner = (jnp.dot(tri, hi, preferred_element_type=F32)
                 + jnp.dot(tri, mid, preferred_element_type=F32)
                 + jnp.dot(tri, lo, preferred_element_type=F32))
        carry = carry_ref[...]
        w = jnp.exp(ls + (carry + inner))
        carry_ref[...] = carry + (inner[0:1, :] + lk[0:1, :])
        vb = vb_refs[j][...]
        for h in range(h_b):
            wv = w[:, h:h + 1] * vb[:, h * D_HEAD:(h + 1) * D_HEAD]
            acc_ref[h] += jnp.sum(wv.reshape(page // 8, 8, D_HEAD), axis=0)

    @pl.when(s == n_steps - 1)
    def _():
        for h in range(h_b):
            ob_ref[h:h + 1, :] = jnp.sum(acc_ref[h], axis=0, keepdims=True)
        kmean = ksum_ref[...] * (1.0 / MOBA_BLOCK)
        rta = head_rows(qa_ref[...], wa)
        gate = _nt_dot(kmean, rta, precision=lax.Precision.HIGHEST)
        rowf = lax.broadcasted_iota(jnp.int32, gate.shape, 0).astype(F32)
        picks = []
        for _k in range(MOBA_TOPK):
            mx = jnp.max(gate, axis=0, keepdims=True)
            idx = jnp.min(jnp.where(gate == mx, rowf, float(nblk)), axis=0, keepdims=True)
            picks.append(idx.astype(jnp.int32))
            gate = jnp.where(rowf == idx, NEG_INF, gate)
        pad = jnp.zeros((8 - MOBA_TOPK, D_HEAD), jnp.int32)
        ids_ref[...] = jnp.concatenate(picks + [pad], axis=0)


def _stream_sample(page_table, cache_k_a, cache_k_b, cache_v_b, qa_s, qb_s, layer, pps=4):
    bd, n_pages = page_table.shape
    depth, n_pool, page, h_a, _ = cache_k_a.shape
    h_b = cache_k_b.shape[3]
    wa, wb = h_a * D_HEAD, h_b * D_HEAD
    assert n_pages % pps == 0 and pps % 2 == 0 and (page * 2) == MOBA_BLOCK
    n_steps = n_pages // pps
    nblk = n_pages // 2
    assert nblk >= MOBA_TOPK
    ka2 = cache_k_a.reshape(depth, n_pool, page, wa)
    kb2 = cache_k_b.reshape(depth, n_pool, page, wb)
    vb2 = cache_v_b.reshape(depth, n_pool, page, wb)
    r = jnp.arange(page)
    tri = (r[None, :] > r[:, None]).astype(BF16)

    def page_spec(width, j):
        def imap(b, s, pt):
            return (layer, pt[b * n_pages + (n_pages - 1 - (s * pps + j))], 0, 0)
        return pl.BlockSpec((None, None, page, width), imap)

    in_specs = ([page_spec(wa, j) for j in range(pps)]
                + [page_spec(wb, j) for j in range(pps)]
                + [page_spec(wb, j) for j in range(pps)]
                + [pl.BlockSpec((None, 1, wa), lambda b, s, pt: (b, 0, 0)),
                   pl.BlockSpec((None, 1, wb), lambda b, s, pt: (b, 0, 0)),
                   pl.BlockSpec((page, page), lambda b, s, pt: (0, 0))])
    out_specs = [pl.BlockSpec((None, 8, D_HEAD), lambda b, s, pt: (b, 0, 0)),
                 pl.BlockSpec((None, h_b, D_HEAD), lambda b, s, pt: (b, 0, 0))]
    kern = functools.partial(_stream_kernel, pps=pps, n_steps=n_steps, h_a=h_a, h_b=h_b,
                             scale=D_HEAD ** -0.5)
    ids, ob = pl.pallas_call(
        kern,
        grid_spec=pltpu.PrefetchScalarGridSpec(
            num_scalar_prefetch=1,
            grid=(bd, n_steps),
            in_specs=in_specs,
            out_specs=out_specs,
            scratch_shapes=[pltpu.VMEM((nblk, wa), F32),
                            pltpu.VMEM((1, D_HEAD), F32),
                            pltpu.VMEM((h_b, 8, D_HEAD), F32)]),
        out_shape=[jax.ShapeDtypeStruct((bd, 8, D_HEAD), jnp.int32),
                   jax.ShapeDtypeStruct((bd, h_b, D_HEAD), F32)],
        compiler_params=_cparams(("arbitrary", "arbitrary"), 40),
    )(page_table.reshape(-1), *([ka2] * pps), *([kb2] * pps), *([vb2] * pps),
      qa_s.reshape(bd, 1, wa), qb_s.reshape(bd, 1, wb), tri)
    return ids, ob


def _moba_s_kernel(pt_ref, ids_ref, *refs, n_pg, scale):
    del pt_ref, ids_ref
    k_refs = refs[0:n_pg]
    v_refs = refs[n_pg:2 * n_pg]
    q_ref, kn_ref, vn_ref, o_ref = refs[2 * n_pg:]
    q = q_ref[...]
    q8 = jnp.broadcast_to(q, (8, D_HEAD)).astype(BF16)
    s_self = jnp.sum(q * kn_ref[...], axis=1, keepdims=True) * scale
    scores = [(_nt_dot(q8, k_refs[j][...].astype(BF16)) * scale)[0:1, :] for j in range(n_pg)]
    m = s_self
    for sc in scores:
        m = jnp.maximum(m, jnp.max(sc, axis=1, keepdims=True))
    p_self = jnp.exp(s_self - m)
    l = p_self
    acc = p_self * vn_ref[...]
    for j, sc in enumerate(scores):
        p = jnp.exp(sc - m)
        l = l + jnp.sum(p, axis=1, keepdims=True)
        p8 = jnp.broadcast_to(p, (8, p.shape[1])).astype(BF16)
        acc = acc + jnp.dot(p8, v_refs[j][...].astype(BF16),
                            preferred_element_type=F32)[0:1, :]
    o_ref[...] = acc / l


def _moba_sample(page_table, ids, cache_k_a, cache_v_a, qa_s, ka_s, va_s, layer):
    bd, n_pages = page_table.shape
    depth, n_pool, page, h_a, _ = cache_k_a.shape
    wa = h_a * D_HEAD
    ppb = MOBA_BLOCK // page
    n_pg = MOBA_TOPK * ppb
    ka2 = cache_k_a.reshape(depth, n_pool, page, wa)
    va2 = cache_v_a.reshape(depth, n_pool, page, wa)
    ids_flat = ids[:, :MOBA_TOPK, :h_a].transpose(0, 2, 1).reshape(-1)

    def page_spec(j):
        kk, half = divmod(j, ppb)

        def imap(b, h, pt, idf):
            blk_id = idf[(b * h_a + h) * MOBA_TOPK + kk]
            return (layer, pt[b * n_pages + blk_id * ppb + half], 0, h)
        return pl.BlockSpec((None, None, page, D_HEAD), imap)

    vec = pl.BlockSpec((None, None, 1, D_HEAD), lambda b, h, pt, idf: (b, h, 0, 0))
    kern = functools.partial(_moba_s_kernel, n_pg=n_pg, scale=D_HEAD ** -0.5)
    out = pl.pallas_call(
        kern,
        grid_spec=pltpu.PrefetchScalarGridSpec(
            num_scalar_prefetch=2,
            grid=(bd, h_a),
            in_specs=[page_spec(j) for j in range(n_pg)] * 2 + [vec, vec, vec],
            out_specs=vec),
        out_shape=jax.ShapeDtypeStruct((bd, h_a, 1, D_HEAD), F32),
        compiler_params=_cparams(("arbitrary", "arbitrary"), 32),
    )(page_table.reshape(-1), ids_flat, *([ka2] * n_pg), *([va2] * n_pg),
      qa_s.reshape(bd, h_a, 1, D_HEAD), ka_s.reshape(bd, h_a, 1, D_HEAD),
      va_s.reshape(bd, h_a, 1, D_HEAD))
    return out.reshape(bd, wa)


def _post_s_kernel(oa_ref, ga_ref, ob_ref, gb_ref, u_ref, gc_ref, st_ref, wp_ref, sc_ref,
                   wa_ref, wb_ref, wc_ref, x_ref, g_ref, b_ref,
                   of_ref, obf_ref, ns_ref, *, alpha, pos):
    cg = D_HEAD
    u = u_ref[...]
    outs = []
    for gi, w in enumerate(POOL_WINDOWS):
        sl = slice(gi * cg, (gi + 1) * cg)
        win = u[:, sl]
        for d in range(1, w):
            win = win + st_ref[POOL_BUF - d][:, sl]
        r = win / float(min(pos + 1, w)) - u[:, sl]
        outs.append(jnp.dot(r.astype(BF16), wp_ref[gi].astype(BF16),
                            preferred_element_type=F32))
    oc = jnp.concatenate(outs, axis=1) * sc_ref[...]
    ma = (oa_ref[...] * _silu(ga_ref[...])).astype(BF16)
    mb = (ob_ref[...] * _silu(gb_ref[...])).astype(BF16)
    mc = (oc * _silu(gc_ref[...])).astype(BF16)
    y = (alpha * x_ref[...]
         + jnp.dot(ma, wa_ref[...], preferred_element_type=F32)
         + jnp.dot(mb, wb_ref[...], preferred_element_type=F32)
         + jnp.dot(mc, wc_ref[...], preferred_element_type=F32))
    out = _layer_norm(y, g_ref[...], b_ref[...])
    of_ref[...] = out
    obf_ref[...] = out.astype(BF16)
    for d in range(POOL_BUF - 1):
        ns_ref[d] = st_ref[d + 1]
    ns_ref[POOL_BUF - 1] = u


def _post_sample(oa, ga, ob, gb, uc, gc, state_t, w_pool, pool_scale, w_out_bf, xs,
                 ln_g, ln_b, layer, alpha, pos):
    r, d = xs.shape
    wa, wb, wc = oa.shape[1], ob.shape[1], uc.shape[1]
    kern = functools.partial(_post_s_kernel, alpha=alpha, pos=pos)
    full = lambda a: pl.BlockSpec(a.shape, lambda i: (0,) * a.ndim)
    vec = pl.BlockSpec((None, 1, d), lambda i: (layer, 0, 0))
    return pl.pallas_call(
        kern,
        grid=(1,),
        in_specs=[full(oa), full(ga), full(ob), full(gb), full(uc), full(gc),
                  pl.BlockSpec((None,) + state_t.shape[1:], lambda i: (layer, 0, 0, 0)),
                  pl.BlockSpec((None,) + w_pool.shape[1:], lambda i: (layer, 0, 0, 0)),
                  pl.BlockSpec((None, 1, wc), lambda i: (layer, 0, 0)),
                  pl.BlockSpec((None, wa, d), lambda i: (layer, 0, 0)),
                  pl.BlockSpec((None, wb, d), lambda i: (layer, wa // wb, 0)),
                  pl.BlockSpec((None, wc, d), lambda i: (layer, wa // wb + 1, 0)),
                  full(xs), vec, vec],
        out_specs=[pl.BlockSpec((r, d), lambda i: (0, 0)), pl.BlockSpec((r, d), lambda i: (0, 0)),
                   pl.BlockSpec((POOL_BUF, r, wc), lambda i: (0, 0, 0))],
        out_shape=[jax.ShapeDtypeStruct((r, d), F32), jax.ShapeDtypeStruct((r, d), BF16),
                   jax.ShapeDtypeStruct((POOL_BUF, r, wc), F32)],
        compiler_params=_cparams(("arbitrary",), 48),
    )(oa, ga, ob, gb, uc, gc, state_t, w_pool,
      pool_scale.reshape(pool_scale.shape[0], 1, wc), w_out_bf, w_out_bf, w_out_bf, xs,
      ln_g.reshape(ln_g.shape[0], 1, d), ln_b.reshape(ln_b.shape[0], 1, d))


def _rope_tables(pos):
    half = D_HEAD // 2
    inv = 1.0 / (ROPE_THETA ** (jnp.arange(half, dtype=F32) * (2.0 / D_HEAD)))
    ang = pos.astype(F32)[:, None] * inv[None, :]
    cos, sin = jnp.cos(ang), jnp.sin(ang)
    return jnp.concatenate([cos, cos], axis=1), jnp.concatenate([-sin, sin], axis=1)


def kernel(x_prompt, x_sample, cache_k_a, cache_v_a, cache_k_b, cache_v_b, state_pool,
           page_table, w_in, w_pool, pool_scale, w_out, ln_g, ln_b):
    b, t, d = x_prompt.shape
    bd, ts, _ = x_sample.shape
    assert ts == 1
    depth = w_in.shape[0]
    h_a, h_b = cache_k_a.shape[3], cache_k_b.shape[3]
    wa, wb = h_a * D_HEAD, h_b * D_HEAD
    wc = state_pool.shape[3]
    assert w_in.shape[2] == 4 * wa + 4 * wb + 2 * wc
    n_pages = page_table.shape[1]
    past_len = n_pages * cache_k_a.shape[2]
    alpha = (2 * depth) ** 0.25
    m = b * t

    w_in_bf = w_in.astype(BF16)
    w_out_bf = w_out.astype(BF16)
    cos_p, sin_p = _rope_tables(jnp.arange(t, dtype=jnp.int32))
    cos_s, sin_s = _rope_tables(jnp.full((bd,), past_len, jnp.int32))
    state_t = state_pool.transpose(0, 2, 1, 3)

    c_qa, c_ka, c_va, c_ga = 0, wa, 2 * wa, 3 * wa
    c_qb = 4 * wa
    c_kb, c_vb, c_gb = c_qb + wb, c_qb + 2 * wb, c_qb + 3 * wb
    c_uc = c_qb + 4 * wb
    c_gc = c_uc + wc

    xp = x_prompt.reshape(m, d)
    xp_bf = xp.astype(BF16)
    xs = x_sample.reshape(bd, d)
    xs_bf = xs.astype(BF16)

    pka, pva, pkb, pvb, ppool = [], [], [], [], []
    ska, sva, skb, svb, spool = [], [], [], [], []
    for l in range(depth):
        proj = functools.partial(_proj, xp_bf, w_in_bf, l, seq_len=t)
        (qa,) = proj(c_qa, wa, cos_t=cos_p, sin_t=sin_p)
        ka, ka_bf, kmean = proj(c_ka, wa, cos_t=cos_p, sin_t=sin_p, emit_bf16=True, emit_mean=True)
        va, va_bf = proj(c_va, wa, emit_bf16=True)
        (ga,) = proj(c_ga, wa)
        (qb_bf,) = proj(c_qb, wb, emit_f32=False, emit_bf16=True)
        kb, kb_bf = proj(c_kb, wb, emit_bf16=True)
        vb, vb_bf = proj(c_vb, wb, emit_bf16=True)
        (gb,) = proj(c_gb, wb)
        (uc,) = proj(c_uc, wc)
        (gc,) = proj(c_gc, wc)

        r3 = lambda a: a.reshape(b, t, a.shape[-1])
        mix_a = _moba_prompt(r3(qa), r3(ka_bf), r3(va_bf),
                             kmean.reshape(b, t // MOBA_BLOCK, wa), r3(ga))
        mix_b = _sb_prompt(r3(qb_bf), r3(kb_bf), r3(vb_bf), r3(gb))
        mix_c = _pool_prompt(r3(uc), r3(gc), w_pool, pool_scale, l)
        xp, xp_bf = _merge(mix_a.reshape(m, wa), mix_b.reshape(m, wb), mix_c.reshape(m, wc),
                           w_out_bf, xp, ln_g, ln_b, l, alpha)
        pka.append(ka.reshape(b, t, h_a, D_HEAD))
        pva.append(va.reshape(b, t, h_a, D_HEAD))
        pkb.append(kb.reshape(b, t, h_b, D_HEAD))
        pvb.append(vb.reshape(b, t, h_b, D_HEAD))
        ppool.append(r3(uc)[:, t - POOL_BUF:, :])

        hs = _proj_sample(xs_bf, w_in_bf, l, cos_s, sin_s, 2 * wa)
        qa_s, ka_s = hs[:, c_qa:c_qa + wa], hs[:, c_ka:c_ka + wa]
        va_s, ga_s = hs[:, c_va:c_va + wa], hs[:, c_ga:c_ga + wa]
        qb_s, kb_s = hs[:, c_qb:c_qb + wb], hs[:, c_kb:c_kb + wb]
        vb_s, gb_s = hs[:, c_vb:c_vb + wb], hs[:, c_gb:c_gb + wb]
        uc_s, gc_s = hs[:, c_uc:c_uc + wc], hs[:, c_gc:c_gc + wc]
        ids, ob_s = _stream_sample(page_table, cache_k_a, cache_k_b, cache_v_b, qa_s, qb_s, l)
        oa_s = _moba_sample(page_table, ids, cache_k_a, cache_v_a, qa_s, ka_s, va_s, l)
        xs, xs_bf, new_state = _post_sample(
            oa_s, ga_s, ob_s.reshape(bd, wb), gb_s, uc_s, gc_s, state_t, w_pool, pool_scale,
            w_out_bf, xs, ln_g, ln_b, l, alpha, past_len)
        ska.append(ka_s.reshape(bd, 1, h_a, D_HEAD))
        sva.append(va_s.reshape(bd, 1, h_a, D_HEAD))
        skb.append(kb_s.reshape(bd, 1, h_b, D_HEAD))
        svb.append(vb_s.reshape(bd, 1, h_b, D_HEAD))
        spool.append(new_state.transpose(1, 0, 2))

    return (xp.reshape(b, t, d), xs.reshape(bd, 1, d),
            jnp.stack(pka), jnp.stack(pva), jnp.stack(pkb), jnp.stack(pvb), jnp.stack(ppool),
            jnp.stack(ska), jnp.stack(sva), jnp.stack(skb), jnp.stack(svb), jnp.stack(spool))
```

```python
import functools

import jax
import jax.numpy as jnp
from jax import lax
from jax.experimental import pallas as pl
from jax.experimental.pallas import tpu as pltpu

F32 = jnp.float32
BF16 = jnp.bfloat16
NEG_INF = float("-inf")

D_HEAD = 128
MOBA_BLOCK = 256
MOBA_TOPK = 3
POOL_WINDOWS = (2, 4, 8, 16)
POOL_BUF = max(POOL_WINDOWS) - 1
ROPE_THETA = 10000.0
LN_EPS = 1e-5

_MIB = 1024 * 1024


def _cparams(semantics, vmem_mib):
    return pltpu.CompilerParams(dimension_semantics=semantics,
                                vmem_limit_bytes=vmem_mib * _MIB)


_NT = (((1,), (1,)), ((), ()))


def _nt_dot(a, b, precision=None):
    return lax.dot_general(a, b, _NT, precision=precision, preferred_element_type=F32)


def _dot(a, b):
    return jnp.dot(a, b, preferred_element_type=F32)


def _split2(x):
    hi = x.astype(BF16)
    lo = (x - hi.astype(F32)).astype(BF16)
    return hi, lo


def _log_sigmoid_pair(z):
    t = jnp.log1p(jnp.exp(-jnp.abs(z)))
    return jnp.minimum(z, 0.0) - t, jnp.minimum(-z, 0.0) - t


def _silu(g):
    return g * jax.nn.sigmoid(g)


def _half(i):
    return lax.shift_right_logical(i, 1)


def _proj_kernel(*refs, rotary, emit_f32, emit_bf16, emit_mean, n_heads_tile):
    it = iter(refs)
    x_ref = next(it)
    w_ref = next(it)
    cos_ref = next(it) if rotary else None
    sin_ref = next(it) if rotary else None
    of_ref = next(it) if emit_f32 else None
    ob_ref = next(it) if emit_bf16 else None
    om_ref = next(it) if emit_mean else None

    acc = _dot(x_ref[...], w_ref[...])
    if rotary:
        cos = cos_ref[...]
        sin = sin_ref[...]
        parts = []
        for j in range(n_heads_tile):
            xh = acc[:, j * D_HEAD:(j + 1) * D_HEAD]
            parts.append(xh * cos + pltpu.roll(xh, D_HEAD // 2, axis=1) * sin)
        acc = jnp.concatenate(parts, axis=1) if len(parts) > 1 else parts[0]
    if emit_f32:
        of_ref[...] = acc
    if emit_bf16:
        ob_ref[...] = acc.astype(BF16)
    if emit_mean:
        tm, tn = acc.shape
        g = tm // MOBA_BLOCK
        om_ref[...] = jnp.sum(acc.reshape(g, MOBA_BLOCK, tn), axis=1) * (1.0 / MOBA_BLOCK)


def _proj(x_bf, w_bf, layer, col0, ncols, *, seq_len, cos_t=None, sin_t=None,
          emit_f32=True, emit_bf16=False, emit_mean=False, tm=512, tn=512):
    m, k = x_bf.shape
    tn = min(tn, ncols)
    assert m % tm == 0 and ncols % tn == 0 and col0 % tn == 0 and seq_len % tm == 0
    assert tm % MOBA_BLOCK == 0
    rotary = cos_t is not None
    n_m, n_n = m // tm, ncols // tn
    nb0 = col0 // tn
    pos_tiles = seq_len // tm

    in_specs = [
        pl.BlockSpec((tm, k), lambda n, i: (i, 0)),
        pl.BlockSpec((None, k, tn), lambda n, i: (layer, 0, nb0 + n)),
    ]
    args = [x_bf, w_bf]
    if rotary:
        in_specs += [pl.BlockSpec((tm, D_HEAD), lambda n, i: (i % pos_tiles, 0))] * 2
        args += [cos_t, sin_t]
    out_shape, out_specs = [], []
    if emit_f32:
        out_shape.append(jax.ShapeDtypeStruct((m, ncols), F32))
        out_specs.append(pl.BlockSpec((tm, tn), lambda n, i: (i, n)))
    if emit_bf16:
        out_shape.append(jax.ShapeDtypeStruct((m, ncols), BF16))
        out_specs.append(pl.BlockSpec((tm, tn), lambda n, i: (i, n)))
    if emit_mean:
        g = tm // MOBA_BLOCK
        out_shape.append(jax.ShapeDtypeStruct((n_m, g, ncols), F32))
        out_specs.append(pl.BlockSpec((None, g, tn), lambda n, i: (i, 0, n)))
    kern = functools.partial(_proj_kernel, rotary=rotary, emit_f32=emit_f32,
                             emit_bf16=emit_bf16, emit_mean=emit_mean,
                             n_heads_tile=tn // D_HEAD)
    return pl.pallas_call(
        kern,
        grid=(n_n, n_m),
        in_specs=in_specs,
        out_specs=out_specs,
        out_shape=out_shape,
        compiler_params=_cparams(("arbitrary", "arbitrary"), 48),
    )(*args)


def _grouped_sweep(count, group, carry, block_of):
    quads = lax.shift_right_logical(count, 2)
    carry = lax.fori_loop(
        0, quads, lambda j, c: group([block_of(4 * j + k) for k in range(4)], c), carry)
    done = 4 * quads
    has_pair = jnp.bitwise_and(count, 2)
    carry = lax.cond(has_pair != 0,
                     lambda c: group([block_of(done), block_of(done + 1)], c),
                     lambda c: c, carry)
    done = done + has_pair
    return lax.cond(jnp.bitwise_and(count, 1) != 0,
                    lambda c: group([block_of(done)], c), lambda c: c, carry)


def _fill_kv(k_ref, v_ref, kb_scr, vt_scr):
    nblk, blk, _ = kb_scr.shape
    for c in range(nblk):
        rows = slice(c * blk, (c + 1) * blk)
        kb_scr[c] = k_ref[rows, :].astype(BF16)
        vt_scr[c] = v_ref[rows, :].T.astype(BF16)


def _moba_kernel(q_ref, k_ref, v_ref, km_ref, g_ref, o_ref, kb_scr, vt_scr, sel_scr, *, scale):
    i = pl.program_id(2)
    blk = MOBA_BLOCK

    @pl.when(i == 0)
    def _():
        _fill_kv(k_ref, v_ref, kb_scr, vt_scr)

    q = q_ref[...]
    qb = q.astype(BF16)
    km = km_ref[...]
    nb = km.shape[0]

    gate = _nt_dot(km, q, precision=lax.Precision.HIGHEST)
    rowi = lax.broadcasted_iota(jnp.int32, gate.shape, 0)
    past = rowi < i
    gate = jnp.where(past, gate, NEG_INF)
    cnt = jnp.zeros(gate.shape, jnp.int32)
    for j in range(nb):
        gj = gate[j:j + 1, :]
        cnt = cnt + jnp.where(gj > gate, 1, jnp.where(gj == gate, jnp.where(rowi > j, 1, 0), 0))
    sel_scr[...] = jnp.where(past, jnp.where(cnt < MOBA_TOPK, 1.0, 0.0), 0.0)

    keyi = lax.broadcasted_iota(jnp.int32, (blk, blk), 0)
    qi = lax.broadcasted_iota(jnp.int32, (blk, blk), 1)

    def scores(n):
        return _nt_dot(kb_scr[n], qb) * scale

    def masked(n):
        return jnp.where(sel_scr[pl.ds(n, 1), :] > 0.0, scores(n), NEG_INF)

    s = jnp.where(keyi <= qi, scores(i), NEG_INF)
    m = jnp.max(s, axis=0, keepdims=True)
    p = jnp.exp(s - m)
    l = jnp.sum(p, axis=0, keepdims=True)
    acc = _dot(vt_scr[i], p.astype(BF16))

    def group(ns, carry):
        m, l, acc = carry
        ss = [masked(n) for n in ns]
        m_new = m
        for s in ss:
            m_new = jnp.maximum(m_new, jnp.max(s, axis=0, keepdims=True))
        l = jnp.exp(m - m_new) * l
        acc = jnp.exp(m - m_new) * acc
        for n, s in zip(ns, ss):
            p = jnp.exp(s - m_new)
            l = l + jnp.sum(p, axis=0, keepdims=True)
            acc = acc + _dot(vt_scr[n], p.astype(BF16))
        return m_new, l, acc

    m, l, acc = _grouped_sweep(i, group, (m, l, acc), lambda v: v)
    o = (acc / l).T
    o_ref[...] = (o * _silu(g_ref[...])).astype(BF16)


def _moba_prompt(qa, ka, va, kmean, ga):
    b, t, wa = qa.shape
    h = wa // D_HEAD
    blk = MOBA_BLOCK
    nq = t // blk
    nb = kmean.shape[1]
    kern = functools.partial(_moba_kernel, scale=D_HEAD ** -0.5)
    tile = pl.BlockSpec((None, blk, D_HEAD), lambda bi, hi, i: (bi, i, hi))
    full = pl.BlockSpec((None, t, D_HEAD), lambda bi, hi, i: (bi, 0, hi))
    return pl.pallas_call(
        kern,
        grid=(b, h, nq),
        in_specs=[tile, full, full,
                  pl.BlockSpec((None, nb, D_HEAD), lambda bi, hi, i: (bi, 0, hi)),
                  tile],
        out_specs=tile,
        out_shape=jax.ShapeDtypeStruct((b, t, wa), BF16),
        scratch_shapes=[pltpu.VMEM((nq, blk, D_HEAD), BF16),
                        pltpu.VMEM((nq, D_HEAD, blk), BF16),
                        pltpu.VMEM((nb, blk), F32)],
        compiler_params=_cparams(("arbitrary",) * 3, 32),
    )(qa, ka, va, kmean, ga)


def _sb_kernel(q_ref, k_ref, v_ref, tri_ref, g_ref, o_ref, kb_scr, vt_scr, *, scale):
    i = pl.program_id(2)
    blk = q_ref.shape[0]

    @pl.when(i == 0)
    def _():
        _fill_kv(k_ref, v_ref, kb_scr, vt_scr)

    qb = q_ref[...]
    tri = tri_ref[...]
    keyi = lax.broadcasted_iota(jnp.int32, (blk, blk), 0)
    qi = lax.broadcasted_iota(jnp.int32, (blk, blk), 1)

    def terms(n, mask):
        z = _nt_dot(kb_scr[n], qb) * scale
        ls, lk = _log_sigmoid_pair(z)
        if mask is not None:
            lk = jnp.where(mask, lk, 0.0)
        hi, lo = _split2(lk)
        inner = _dot(tri, hi) + _dot(tri, lo)
        return ls, inner, inner[0:1, :] + lk[0:1, :]

    diag = keyi < qi
    ls, inner, carry = terms(i, diag)
    w = jnp.where(diag, jnp.exp(ls + inner), 0.0)
    acc = _dot(vt_scr[i], w.astype(BF16))

    def group(ns, c):
        carry, acc = c
        parts = [terms(n, None) for n in ns]
        for n, (ls, inner, tot) in zip(ns, parts):
            w = jnp.exp(ls + (carry + inner))
            acc = acc + _dot(vt_scr[n], w.astype(BF16))
            carry = carry + tot
        return carry, acc

    carry, acc = _grouped_sweep(i, group, (carry, acc), lambda v: i - 1 - v)
    o_ref[...] = (acc.T * _silu(g_ref[...])).astype(BF16)


def _sb_prompt(qb_bf, kb, vb, gb, blk=256):
    b, t, wb = qb_bf.shape
    h = wb // D_HEAD
    nq = t // blk
    r = jnp.arange(blk)
    tri = (r[None, :] > r[:, None]).astype(BF16)
    kern = functools.partial(_sb_kernel, scale=D_HEAD ** -0.5)
    tile = pl.BlockSpec((None, blk, D_HEAD), lambda bi, hi, i: (bi, i, hi))
    full = pl.BlockSpec((None, t, D_HEAD), lambda bi, hi, i: (bi, 0, hi))
    return pl.pallas_call(
        kern,
        grid=(b, h, nq),
        in_specs=[tile, full, full,
                  pl.BlockSpec((blk, blk), lambda bi, hi, i: (0, 0)),
                  tile],
        out_specs=tile,
        out_shape=jax.ShapeDtypeStruct((b, t, wb), BF16),
        scratch_shapes=[pltpu.VMEM((nq, blk, D_HEAD), BF16),
                        pltpu.VMEM((nq, D_HEAD, blk), BF16)],
        compiler_params=_cparams(("arbitrary",) * 3, 32),
    )(qb_bf, kb, vb, tri, gb)


def _pool_kernel(u_ref, halo_ref, wp_ref, sc_ref, g_ref, o_ref, xp_ref, *, tq):
    i = pl.program_id(1)
    cg = D_HEAD
    halo = halo_ref[...]
    xp_ref[0:POOL_BUF + 1, :] = jnp.where(i > 0, halo, 0.0)
    xp_ref[POOL_BUF + 1:, :] = u_ref[...]
    pos = i * tq + lax.broadcasted_iota(jnp.int32, (tq, 1), 0)
    outs = []
    for gi, w in enumerate(POOL_WINDOWS):
        sl = slice(gi * cg, (gi + 1) * cg)
        x = xp_ref[POOL_BUF + 1:, sl]
        win = x
        for d in range(1, w):
            win = win + xp_ref[pl.ds(POOL_BUF + 1 - d, tq), sl]
        cnt = jnp.minimum(pos + 1, w).astype(F32)
        r = win / cnt - x
        outs.append(_dot(r.astype(BF16), wp_ref[gi].astype(BF16)))
    mixed = jnp.concatenate(outs, axis=1) * sc_ref[...]
    o_ref[...] = (mixed * _silu(g_ref[...])).astype(BF16)


def _pool_prompt(uc, gc, w_pool, pool_scale, layer, tq=512):
    b, t, wc = uc.shape
    hb = POOL_BUF + 1
    assert t % tq == 0 and tq % hb == 0
    halo_per_tile = tq // hb
    kern = functools.partial(_pool_kernel, tq=tq)
    tile = pl.BlockSpec((None, tq, wc), lambda bi, i: (bi, i, 0))
    return pl.pallas_call(
        kern,
        grid=(b, t // tq),
        in_specs=[tile,
                  pl.BlockSpec((None, hb, wc),
                               lambda bi, i: (bi, jnp.maximum(i * halo_per_tile - 1, 0), 0)),
                  pl.BlockSpec((None,) + w_pool.shape[1:], lambda bi, i: (layer, 0, 0, 0)),
                  pl.BlockSpec((None, 1, wc), lambda bi, i: (layer, 0, 0)),
                  tile],
        out_specs=tile,
        out_shape=jax.ShapeDtypeStruct((b, t, wc), BF16),
        scratch_shapes=[pltpu.VMEM((tq + hb, wc), F32)],
        compiler_params=_cparams(("arbitrary", "arbitrary"), 32),
    )(uc, uc, w_pool, pool_scale.reshape(pool_scale.shape[0], 1, wc), gc)


def _layer_norm(y, g, b):
    mu = jnp.mean(y, axis=-1, keepdims=True)
    d = y - mu
    var = jnp.mean(d * d, axis=-1, keepdims=True)
    return d * lax.rsqrt(var + LN_EPS) * g + b


def _merge_kernel(ma_ref, mb_ref, mc_ref, wa_ref, wb_ref, wc_ref, x_ref, g_ref, b_ref,
                  of_ref, ob_ref, *, alpha):
    y = (alpha * x_ref[...]
         + _dot(ma_ref[...], wa_ref[...])
         + _dot(mb_ref[...], wb_ref[...])
         + _dot(mc_ref[...], wc_ref[...]))
    out = _layer_norm(y, g_ref[...], b_ref[...])
    of_ref[...] = out
    ob_ref[...] = out.astype(BF16)


def _merge(mix_a, mix_b, mix_c, w_out_bf, x, ln_g, ln_b, layer, alpha, tm=256):
    m, d = x.shape
    wa, wb, wc = mix_a.shape[1], mix_b.shape[1], mix_c.shape[1]
    assert wb == wc and wa % wb == 0
    kern = functools.partial(_merge_kernel, alpha=alpha)
    vec = pl.BlockSpec((None, 1, d), lambda i: (layer, 0, 0))
    row = lambda w: pl.BlockSpec((tm, w), lambda i: (i, 0))
    return pl.pallas_call(
        kern,
        grid=(m // tm,),
        in_specs=[row(wa), row(wb), row(wc),
                  pl.BlockSpec((None, wa, d), lambda i: (layer, 0, 0)),
                  pl.BlockSpec((None, wb, d), lambda i: (layer, wa // wb, 0)),
                  pl.BlockSpec((None, wc, d), lambda i: (layer, wa // wb + 1, 0)),
                  row(d), vec, vec],
        out_specs=[row(d), row(d)],
        out_shape=[jax.ShapeDtypeStruct((m, d), F32), jax.ShapeDtypeStruct((m, d), BF16)],
        compiler_params=_cparams(("arbitrary",), 48),
    )(mix_a, mix_b, mix_c, w_out_bf, w_out_bf, w_out_bf, x,
      ln_g.reshape(ln_g.shape[0], 1, d), ln_b.reshape(ln_b.shape[0], 1, d))


def _proj_s_kernel(x_ref, w_ref, cos_ref, sin_ref, o_ref, *, rot_tiles):
    n = pl.program_id(0)
    acc = _dot(x_ref[...], w_ref[...])
    cos = cos_ref[...]
    sin = sin_ref[...]
    parts = []
    for j in range(acc.shape[1] // D_HEAD):
        xh = acc[:, j * D_HEAD:(j + 1) * D_HEAD]
        parts.append(xh * cos + pltpu.roll(xh, D_HEAD // 2, axis=1) * sin)
    rot = jnp.concatenate(parts, axis=1)
    o_ref[...] = jnp.where(n < rot_tiles, rot, acc)


def _proj_sample(xs_bf, w_bf, layer, cos_row, sin_row, rot_cols, tn=512):
    r, k = xs_bf.shape
    ncols = w_bf.shape[2]
    kern = functools.partial(_proj_s_kernel, rot_tiles=rot_cols // tn)
    return pl.pallas_call(
        kern,
        grid=(ncols // tn,),
        in_specs=[pl.BlockSpec((r, k), lambda n: (0, 0)),
                  pl.BlockSpec((None, k, tn), lambda n: (layer, 0, n)),
                  pl.BlockSpec((r, D_HEAD), lambda n: (0, 0)),
                  pl.BlockSpec((r, D_HEAD), lambda n: (0, 0))],
        out_specs=pl.BlockSpec((r, tn), lambda n: (0, n)),
        out_shape=jax.ShapeDtypeStruct((r, ncols), F32),
        compiler_params=_cparams(("arbitrary",), 32),
    )(xs_bf, w_bf, cos_row, sin_row)


def _page_rows(cache):
    depth, n_pool, page, heads, dh = cache.shape
    return cache.reshape(depth, n_pool, page * heads, dh)


def _head_rows(ref, h, page, heads):
    return ref[pl.ds(h, page, stride=heads), :]


def _stream_kernel(pt_ref, *refs, pps, n_steps, page, h_a, h_b, scale):
    del pt_ref
    ka_refs = refs[0:pps]
    kb_refs = refs[pps:2 * pps]
    vb_refs = refs[2 * pps:3 * pps]
    qa_ref, qb_ref, tri_ref = refs[3 * pps:3 * pps + 3]
    ids_ref, ob_ref = refs[3 * pps + 3:3 * pps + 5]
    ksum_ref, carry_ref, acc_ref = refs[3 * pps + 5:]
    s = pl.program_id(1)
    wb = h_b * D_HEAD
    nblk = ksum_ref.shape[0]

    @pl.when(s == 0)
    def _():
        ksum_ref[...] = jnp.zeros_like(ksum_ref)
        carry_ref[...] = jnp.zeros_like(carry_ref)
        acc_ref[...] = jnp.zeros_like(acc_ref)

    r = lax.broadcasted_iota(jnp.int32, (D_HEAD, wb), 0)
    c = lax.broadcasted_iota(jnp.int32, (D_HEAD, wb), 1)
    rtb = jnp.where(r == jnp.right_shift(c, 7),
                    jnp.broadcast_to(qb_ref[...], (D_HEAD, wb)), 0.0).astype(BF16)
    tri = tri_ref[...]

    last_page = n_steps * pps - 1
    for j in range(pps):
        p_idx = last_page - (s * pps + j)
        ka = ka_refs[j][...].reshape(page, h_a, D_HEAD)
        ksum_ref[_half(p_idx)] += jnp.sum(ka, axis=0)
        kb = jnp.concatenate([_head_rows(kb_refs[j], h, page, h_b) for h in range(h_b)],
                             axis=1).astype(BF16)
        z = _nt_dot(kb, rtb) * scale
        ls, lk = _log_sigmoid_pair(z)
        hi, lo = _split2(lk)
        inner = _dot(tri, hi) + _dot(tri, lo)
        carry = carry_ref[...]
        w = jnp.exp(ls + (carry + inner))
        carry_ref[...] = carry + (inner[0:1, :] + lk[0:1, :])
        for h in range(h_b):
            wv = w[:, h:h + 1] * _head_rows(vb_refs[j], h, page, h_b)
            acc_ref[h] += jnp.sum(wv.reshape(page // 8, 8, D_HEAD), axis=0)

    @pl.when(s == n_steps - 1)
    def _():
        for h in range(h_b):
            ob_ref[h:h + 1, :] = jnp.sum(acc_ref[h], axis=0, keepdims=True)
        q = qa_ref[...]
        lane = lax.broadcasted_iota(jnp.int32, (h_a, D_HEAD), 1)
        lanef = lane.astype(F32)
        gate = jnp.full((h_a, D_HEAD), NEG_INF, F32)
        for n in range(nblk):
            gn = jnp.sum(ksum_ref[n] * q, axis=1, keepdims=True) * (1.0 / MOBA_BLOCK)
            gate = jnp.where(lane == n, gn, gate)
        ids = jnp.zeros((h_a, D_HEAD), jnp.int32)
        for k in range(MOBA_TOPK):
            mx = jnp.max(gate, axis=1, keepdims=True)
            idx = jnp.min(jnp.where(gate == mx, lanef, float(D_HEAD)), axis=1, keepdims=True)
            ids = jnp.where(lane == k, idx.astype(jnp.int32), ids)
            gate = jnp.where(lanef == idx, NEG_INF, gate)
        ids_ref[...] = ids


def _stream_sample(page_table, cache_k_a, cache_k_b, cache_v_b, qa_s, qb_s, layer, pps=8):
    bd, n_pages = page_table.shape
    page, h_a = cache_k_a.shape[2], cache_k_a.shape[3]
    h_b = cache_k_b.shape[3]
    wa, wb = h_a * D_HEAD, h_b * D_HEAD
    pps = min(pps, n_pages)
    assert n_pages % pps == 0 and pps % 2 == 0 and (page * 2) == MOBA_BLOCK
    n_steps = n_pages // pps
    nblk = n_pages // 2
    assert MOBA_TOPK <= nblk <= D_HEAD
    r = jnp.arange(page)
    tri = (r[None, :] > r[:, None]).astype(BF16)

    def page_spec(heads, j):
        def imap(b, s, pt):
            return (layer, pt[b * n_pages + (n_pages - 1 - (s * pps + j))], 0, 0)
        return pl.BlockSpec((None, None, page * heads, D_HEAD), imap)

    in_specs = ([page_spec(h_a, j) for j in range(pps)]
                + [page_spec(h_b, j) for j in range(pps)]
                + [page_spec(h_b, j) for j in range(pps)]
                + [pl.BlockSpec((None, h_a, D_HEAD), lambda b, s, pt: (b, 0, 0)),
                   pl.BlockSpec((None, 1, wb), lambda b, s, pt: (b, 0, 0)),
                   pl.BlockSpec((page, page), lambda b, s, pt: (0, 0))])
    out_specs = [pl.BlockSpec((None, h_a, D_HEAD), lambda b, s, pt: (b, 0, 0)),
                 pl.BlockSpec((None, h_b, D_HEAD), lambda b, s, pt: (b, 0, 0))]
    kern = functools.partial(_stream_kernel, pps=pps, n_steps=n_steps, page=page,
                             h_a=h_a, h_b=h_b, scale=D_HEAD ** -0.5)
    ids, ob = pl.pallas_call(
        kern,
        grid_spec=pltpu.PrefetchScalarGridSpec(
            num_scalar_prefetch=1,
            grid=(bd, n_steps),
            in_specs=in_specs,
            out_specs=out_specs,
            scratch_shapes=[pltpu.VMEM((nblk, h_a, D_HEAD), F32),
                            pltpu.VMEM((1, D_HEAD), F32),
                            pltpu.VMEM((h_b, 8, D_HEAD), F32)]),
        out_shape=[jax.ShapeDtypeStruct((bd, h_a, D_HEAD), jnp.int32),
                   jax.ShapeDtypeStruct((bd, h_b, D_HEAD), F32)],
        compiler_params=_cparams(("arbitrary", "arbitrary"), 48),
    )(page_table.reshape(-1), *([_page_rows(cache_k_a)] * pps), *([_page_rows(cache_k_b)] * pps),
      *([_page_rows(cache_v_b)] * pps), qa_s.reshape(bd, h_a, D_HEAD), qb_s.reshape(bd, 1, wb), tri)
    return ids, ob


def _moba_s_kernel(pt_ref, ids_ref, *refs, n_pg, page, h_a, scale):
    del pt_ref, ids_ref
    k_refs = refs[0:n_pg]
    v_refs = refs[n_pg:2 * n_pg]
    q_ref, kn_ref, vn_ref, o_ref = refs[2 * n_pg:]
    h = pl.program_id(1)
    q = q_ref[...]
    q8 = jnp.broadcast_to(q, (8, D_HEAD)).astype(BF16)
    s_self = jnp.sum(q * kn_ref[...], axis=1, keepdims=True) * scale
    scores = [(_nt_dot(q8, _head_rows(k_refs[j], h, page, h_a).astype(BF16)) * scale)[0:1, :]
              for j in range(n_pg)]
    m = s_self
    for sc in scores:
        m = jnp.maximum(m, jnp.max(sc, axis=1, keepdims=True))
    p_self = jnp.exp(s_self - m)
    l = p_self
    acc = p_self * vn_ref[...]
    for j, sc in enumerate(scores):
        p = jnp.exp(sc - m)
        l = l + jnp.sum(p, axis=1, keepdims=True)
        p8 = jnp.broadcast_to(p, (8, p.shape[1])).astype(BF16)
        acc = acc + _dot(p8, _head_rows(v_refs[j], h, page, h_a).astype(BF16))[0:1, :]
    o_ref[...] = acc / l


def _moba_sample(page_table, ids, cache_k_a, cache_v_a, qa_s, ka_s, va_s, layer):
    bd, n_pages = page_table.shape
    page, h_a = cache_k_a.shape[2], cache_k_a.shape[3]
    wa = h_a * D_HEAD
    ppb = MOBA_BLOCK // page
    n_pg = MOBA_TOPK * ppb
    ids_flat = ids[:, :, :MOBA_TOPK].reshape(-1)

    def page_spec(j):
        kk, half = divmod(j, ppb)

        def imap(b, h, pt, idf):
            blk_id = idf[(b * h_a + h) * MOBA_TOPK + kk]
            return (layer, pt[b * n_pages + blk_id * ppb + half], 0, 0)
        return pl.BlockSpec((None, None, page * h_a, D_HEAD), imap)

    vec = pl.BlockSpec((None, None, 1, D_HEAD), lambda b, h, pt, idf: (b, h, 0, 0))
    kern = functools.partial(_moba_s_kernel, n_pg=n_pg, page=page, h_a=h_a, scale=D_HEAD ** -0.5)
    out = pl.pallas_call(
        kern,
        grid_spec=pltpu.PrefetchScalarGridSpec(
            num_scalar_prefetch=2,
            grid=(bd, h_a),
            in_specs=[page_spec(j) for j in range(n_pg)] * 2 + [vec, vec, vec],
            out_specs=vec),
        out_shape=jax.ShapeDtypeStruct((bd, h_a, 1, D_HEAD), F32),
        compiler_params=_cparams(("arbitrary", "arbitrary"), 32),
    )(page_table.reshape(-1), ids_flat, *([_page_rows(cache_k_a)] * n_pg),
      *([_page_rows(cache_v_a)] * n_pg),
      qa_s.reshape(bd, h_a, 1, D_HEAD), ka_s.reshape(bd, h_a, 1, D_HEAD),
      va_s.reshape(bd, h_a, 1, D_HEAD))
    return out.reshape(bd, wa)


def _post_s_kernel(oa_ref, ga_ref, ob_ref, gb_ref, u_ref, gc_ref, st_ref, wp_ref, sc_ref,
                   wa_ref, wb_ref, wc_ref, x_ref, g_ref, b_ref,
                   of_ref, obf_ref, ns_ref, *, alpha, pos):
    cg = D_HEAD
    u = u_ref[...]
    outs = []
    for gi, w in enumerate(POOL_WINDOWS):
        sl = slice(gi * cg, (gi + 1) * cg)
        win = u[:, sl]
        for d in range(1, w):
            win = win + st_ref[POOL_BUF - d][:, sl]
        r = win / float(min(pos + 1, w)) - u[:, sl]
        outs.append(_dot(r.astype(BF16), wp_ref[gi].astype(BF16)))
    oc = jnp.concatenate(outs, axis=1) * sc_ref[...]
    ma = (oa_ref[...] * _silu(ga_ref[...])).astype(BF16)
    mb = (ob_ref[...] * _silu(gb_ref[...])).astype(BF16)
    mc = (oc * _silu(gc_ref[...])).astype(BF16)
    y = (alpha * x_ref[...]
         + _dot(ma, wa_ref[...])
         + _dot(mb, wb_ref[...])
         + _dot(mc, wc_ref[...]))
    out = _layer_norm(y, g_ref[...], b_ref[...])
    of_ref[...] = out
    obf_ref[...] = out.astype(BF16)
    for d in range(POOL_BUF - 1):
        ns_ref[d] = st_ref[d + 1]
    ns_ref[POOL_BUF - 1] = u


def _post_sample(oa, ga, ob, gb, uc, gc, state_t, w_pool, pool_scale, w_out_bf, xs,
                 ln_g, ln_b, layer, alpha, pos):
    r, d = xs.shape
    wa, wb, wc = oa.shape[1], ob.shape[1], uc.shape[1]
    kern = functools.partial(_post_s_kernel, alpha=alpha, pos=pos)
    full = lambda a: pl.BlockSpec(a.shape, lambda i: (0,) * a.ndim)
    vec = pl.BlockSpec((None, 1, d), lambda i: (layer, 0, 0))
    return pl.pallas_call(
        kern,
        grid=(1,),
        in_specs=[full(oa), full(ga), full(ob), full(gb), full(uc), full(gc),
                  pl.BlockSpec((None,) + state_t.shape[1:], lambda i: (layer, 0, 0, 0)),
                  pl.BlockSpec((None,) + w_pool.shape[1:], lambda i: (layer, 0, 0, 0)),
                  pl.BlockSpec((None, 1, wc), lambda i: (layer, 0, 0)),
                  pl.BlockSpec((None, wa, d), lambda i: (layer, 0, 0)),
                  pl.BlockSpec((None, wb, d), lambda i: (layer, wa // wb, 0)),
                  pl.BlockSpec((None, wc, d), lambda i: (layer, wa // wb + 1, 0)),
                  full(xs), vec, vec],
        out_specs=[pl.BlockSpec((r, d), lambda i: (0, 0)), pl.BlockSpec((r, d), lambda i: (0, 0)),
                   pl.BlockSpec((POOL_BUF, r, wc), lambda i: (0, 0, 0))],
        out_shape=[jax.ShapeDtypeStruct((r, d), F32), jax.ShapeDtypeStruct((r, d), BF16),
                   jax.ShapeDtypeStruct((POOL_BUF, r, wc), F32)],
        compiler_params=_cparams(("arbitrary",), 48),
    )(oa, ga, ob, gb, uc, gc, state_t, w_pool,
      pool_scale.reshape(pool_scale.shape[0], 1, wc), w_out_bf, w_out_bf, w_out_bf, xs,
      ln_g.reshape(ln_g.shape[0], 1, d), ln_b.reshape(ln_b.shape[0], 1, d))


def _rope_tables(pos):
    half = D_HEAD // 2
    inv = 1.0 / (ROPE_THETA ** (jnp.arange(half, dtype=F32) * (2.0 / D_HEAD)))
    ang = pos.astype(F32)[:, None] * inv[None, :]
    cos, sin = jnp.cos(ang), jnp.sin(ang)
    return jnp.concatenate([cos, cos], axis=1), jnp.concatenate([-sin, sin], axis=1)


def kernel(x_prompt, x_sample, cache_k_a, cache_v_a, cache_k_b, cache_v_b, state_pool,
           page_table, w_in, w_pool, pool_scale, w_out, ln_g, ln_b):
    b, t, d = x_prompt.shape
    bd, ts, _ = x_sample.shape
    assert ts == 1
    depth = w_in.shape[0]
    h_a, h_b = cache_k_a.shape[3], cache_k_b.shape[3]
    wa, wb = h_a * D_HEAD, h_b * D_HEAD
    wc = state_pool.shape[3]
    assert w_in.shape[2] == 4 * wa + 4 * wb + 2 * wc
    n_pages = page_table.shape[1]
    past_len = n_pages * cache_k_a.shape[2]
    alpha = (2 * depth) ** 0.25
    m = b * t

    w_in_bf = w_in.astype(BF16)
    w_out_bf = w_out.astype(BF16)
    cos_p, sin_p = _rope_tables(jnp.arange(t, dtype=jnp.int32))
    cos_s, sin_s = _rope_tables(jnp.full((bd,), past_len, jnp.int32))
    state_t = state_pool.transpose(0, 2, 1, 3)

    c_qa, c_ka, c_va, c_ga = 0, wa, 2 * wa, 3 * wa
    c_qb = 4 * wa
    c_kb, c_vb, c_gb = c_qb + wb, c_qb + 2 * wb, c_qb + 3 * wb
    c_uc = c_qb + 4 * wb
    c_gc = c_uc + wc

    xp = x_prompt.reshape(m, d)
    xp_bf = xp.astype(BF16)
    xs = x_sample.reshape(bd, d)
    xs_bf = xs.astype(BF16)

    pka, pva, pkb, pvb, ppool = [], [], [], [], []
    ska, sva, skb, svb, spool = [], [], [], [], []
    for l in range(depth):
        proj = functools.partial(_proj, xp_bf, w_in_bf, l, seq_len=t)
        (qa,) = proj(c_qa, wa, cos_t=cos_p, sin_t=sin_p)
        ka, kmean = proj(c_ka, wa, cos_t=cos_p, sin_t=sin_p, emit_mean=True)
        (va,) = proj(c_va, wa)
        (ga,) = proj(c_ga, wa)
        (qb_bf,) = proj(c_qb, wb, emit_f32=False, emit_bf16=True)
        (kb,) = proj(c_kb, wb)
        (vb,) = proj(c_vb, wb)
        (gb,) = proj(c_gb, wb)
        (uc,) = proj(c_uc, wc)
        (gc,) = proj(c_gc, wc)

        r3 = lambda a: a.reshape(b, t, a.shape[-1])
        mix_a = _moba_prompt(r3(qa), r3(ka), r3(va), kmean.reshape(b, t // MOBA_BLOCK, wa), r3(ga))
        mix_b = _sb_prompt(r3(qb_bf), r3(kb), r3(vb), r3(gb))
        mix_c = _pool_prompt(r3(uc), r3(gc), w_pool, pool_scale, l)
        xp, xp_bf = _merge(mix_a.reshape(m, wa), mix_b.reshape(m, wb), mix_c.reshape(m, wc),
                           w_out_bf, xp, ln_g, ln_b, l, alpha)
        pka.append(ka.reshape(b, t, h_a, D_HEAD))
        pva.append(va.reshape(b, t, h_a, D_HEAD))
        pkb.append(kb.reshape(b, t, h_b, D_HEAD))
        pvb.append(vb.reshape(b, t, h_b, D_HEAD))
        ppool.append(r3(uc)[:, t - POOL_BUF:, :])

        hs = _proj_sample(xs_bf, w_in_bf, l, cos_s, sin_s, 2 * wa)
        qa_s, ka_s = hs[:, c_qa:c_qa + wa], hs[:, c_ka:c_ka + wa]
        va_s, ga_s = hs[:, c_va:c_va + wa], hs[:, c_ga:c_ga + wa]
        qb_s, kb_s = hs[:, c_qb:c_qb + wb], hs[:, c_kb:c_kb + wb]
        vb_s, gb_s = hs[:, c_vb:c_vb + wb], hs[:, c_gb:c_gb + wb]
        uc_s, gc_s = hs[:, c_uc:c_uc + wc], hs[:, c_gc:c_gc + wc]
        ids, ob_s = _stream_sample(page_table, cache_k_a, cache_k_b, cache_v_b, qa_s, qb_s, l)
        oa_s = _moba_sample(page_table, ids, cache_k_a, cache_v_a, qa_s, ka_s, va_s, l)
        xs, xs_bf, new_state = _post_sample(
            oa_s, ga_s, ob_s.reshape(bd, wb), gb_s, uc_s, gc_s, state_t, w_pool, pool_scale,
            w_out_bf, xs, ln_g, ln_b, l, alpha, past_len)
        ska.append(ka_s.reshape(bd, 1, h_a, D_HEAD))
        sva.append(va_s.reshape(bd, 1, h_a, D_HEAD))
        skb.append(kb_s.reshape(bd, 1, h_b, D_HEAD))
        svb.append(vb_s.reshape(bd, 1, h_b, D_HEAD))
        spool.append(new_state.transpose(1, 0, 2))

    return (xp.reshape(b, t, d), xs.reshape(bd, 1, d),
            jnp.stack(pka), jnp.stack(pva), jnp.stack(pkb), jnp.stack(pvb), jnp.stack(ppool),
            jnp.stack(ska), jnp.stack(sva), jnp.stack(skb), jnp.stack(svb), jnp.stack(spool))
```

```python
import functools

import jax
import jax.numpy as jnp
from jax import lax
from jax.experimental import pallas as pl
from jax.experimental.pallas import tpu as pltpu

F32 = jnp.float32
BF16 = jnp.bfloat16
NEG_INF = float("-inf")

D_HEAD = 128
MOBA_BLOCK = 256
MOBA_TOPK = 3
POOL_WINDOWS = (2, 4, 8, 16)
POOL_BUF = max(POOL_WINDOWS) - 1
ROPE_THETA = 10000.0
LN_EPS = 1e-5
LOG2E = 1.4426950408889634

_MIB = 1024 * 1024


def _cparams(semantics, vmem_mib):
    return pltpu.CompilerParams(dimension_semantics=semantics,
                                vmem_limit_bytes=vmem_mib * _MIB)


_NT = (((1,), (1,)), ((), ()))


def _nt_dot(a, b, precision=None):
    return lax.dot_general(a, b, _NT, precision=precision, preferred_element_type=F32)


def _dot(a, b):
    return jnp.dot(a, b, preferred_element_type=F32)


def _split2(x):
    hi = x.astype(BF16)
    lo = (x - hi.astype(F32)).astype(BF16)
    return hi, lo


def _log_sigmoid_pair(z):
    ls = jnp.minimum(z, 0.0) - jnp.log(1.0 + jnp.exp(-jnp.abs(z)))
    return ls, ls - z


def _silu(g):
    return g * jax.nn.sigmoid(g)


def _half(i):
    return lax.shift_right_logical(i, 1)


def _proj_kernel(*refs, rotary, segs):
    it = iter(refs)
    x_ref = next(it)
    w_ref = next(it)
    cos_ref = next(it) if rotary else None
    sin_ref = next(it) if rotary else None
    x = x_ref[...]
    c0 = 0
    for ncols, dtype, emit_mean in segs:
        acc = _dot(x, w_ref[:, c0:c0 + ncols])
        c0 += ncols
        if rotary:
            cos = cos_ref[...]
            sin = sin_ref[...]
            parts = []
            for j in range(ncols // D_HEAD):
                xh = acc[:, j * D_HEAD:(j + 1) * D_HEAD]
                parts.append(xh * cos + pltpu.roll(xh, D_HEAD // 2, axis=1) * sin)
            acc = jnp.concatenate(parts, axis=1)
        next(it)[...] = acc.astype(dtype)
        if emit_mean:
            tm = acc.shape[0]
            g = tm // MOBA_BLOCK
            next(it)[...] = jnp.sum(acc.reshape(g, MOBA_BLOCK, ncols), axis=1) * (1.0 / MOBA_BLOCK)


def _proj(x_bf, w_bf, layer, col0, segs, *, seq_len, cos_t=None, sin_t=None, tm=512):
    m, k = x_bf.shape
    total = sum(s[0] for s in segs)
    assert m % tm == 0 and col0 % total == 0 and seq_len % tm == 0 and tm % MOBA_BLOCK == 0
    rotary = cos_t is not None
    n_m = m // tm
    pos_tiles = seq_len // tm
    g = tm // MOBA_BLOCK

    in_specs = [pl.BlockSpec((tm, k), lambda i: (i, 0)),
                pl.BlockSpec((None, k, total), lambda i: (layer, 0, col0 // total))]
    args = [x_bf, w_bf]
    if rotary:
        in_specs += [pl.BlockSpec((tm, D_HEAD), lambda i: (i % pos_tiles, 0))] * 2
        args += [cos_t, sin_t]
    out_shape, out_specs = [], []
    for ncols, dtype, emit_mean in segs:
        out_shape.append(jax.ShapeDtypeStruct((m, ncols), dtype))
        out_specs.append(pl.BlockSpec((tm, ncols), lambda i: (i, 0)))
        if emit_mean:
            out_shape.append(jax.ShapeDtypeStruct((n_m, g, ncols), F32))
            out_specs.append(pl.BlockSpec((None, g, ncols), lambda i: (i, 0, 0)))
    kern = functools.partial(_proj_kernel, rotary=rotary, segs=tuple(segs))
    return pl.pallas_call(
        kern,
        grid=(n_m,),
        in_specs=in_specs,
        out_specs=out_specs,
        out_shape=out_shape,
        compiler_params=_cparams(("arbitrary",), 56),
    )(*args)


def _grouped_sweep(count, group, carry, block_of):
    quads = lax.shift_right_logical(count, 2)
    carry = lax.fori_loop(
        0, quads, lambda j, c: group([block_of(4 * j + k) for k in range(4)], c), carry)
    done = 4 * quads
    has_pair = jnp.bitwise_and(count, 2)
    carry = lax.cond(has_pair != 0,
                     lambda c: group([block_of(done), block_of(done + 1)], c),
                     lambda c: c, carry)
    done = done + has_pair
    return lax.cond(jnp.bitwise_and(count, 1) != 0,
                    lambda c: group([block_of(done)], c), lambda c: c, carry)


def _early_exit_sweep(count, group, carry, block_of, alive):
    def body(st):
        j, _, c = st
        c = group([block_of(j)], c)
        return j + 1, alive(c), c

    _, _, carry = lax.while_loop(lambda st: jnp.logical_and(st[0] < count, st[1]),
                                 body, (jnp.int32(0), alive(carry), carry))
    return carry


HEADS_PER_STEP = 2

SB_DEAD_LOG = -110.0


def _lanes(h):
    return slice(h * D_HEAD, (h + 1) * D_HEAD)


def _fill_kv(k_ref, v_ref, kb_scr, vt_scr):
    hps, nblk, blk, _ = kb_scr.shape
    for h in range(hps):
        for c in range(nblk):
            rows = slice(c * blk, (c + 1) * blk)
            kb_scr[h, c] = k_ref[rows, _lanes(h)].astype(BF16)
            vt_scr[h, c] = v_ref[rows, _lanes(h)].T.astype(BF16)


def _moba_kernel(q_ref, k_ref, v_ref, km_ref, g_ref, o_ref, kb_scr, vt_scr, sel_scr, *, scale):
    i = pl.program_id(2)
    blk = MOBA_BLOCK
    hps = kb_scr.shape[0]
    nb = km_ref.shape[0]

    @pl.when(i == 0)
    def _():
        _fill_kv(k_ref, v_ref, kb_scr, vt_scr)

    keyi = lax.broadcasted_iota(jnp.int32, (blk, blk), 0)
    qi = lax.broadcasted_iota(jnp.int32, (blk, blk), 1)
    rowi = lax.broadcasted_iota(jnp.int32, (nb, blk), 0)
    past = rowi < i

    qbs, carries = [], []
    for h in range(hps):
        q = q_ref[:, _lanes(h)]
        gate = _nt_dot(km_ref[:, _lanes(h)], q, precision=lax.Precision.HIGHEST)
        gate = jnp.where(past, gate, NEG_INF)
        cnt = jnp.zeros(gate.shape, jnp.int32)
        for j in range(nb):
            gj = gate[j:j + 1, :]
            cnt = cnt + jnp.where(gj > gate, 1,
                                  jnp.where(gj == gate, jnp.where(rowi > j, 1, 0), 0))
        sel_scr[h] = jnp.where(past, jnp.where(cnt < MOBA_TOPK, 1.0, 0.0), 0.0)
        qb = (q * (scale * LOG2E)).astype(BF16)
        qbs.append(qb)
        s = jnp.where(keyi <= qi, _nt_dot(kb_scr[h, i], qb), NEG_INF)
        m = jnp.max(s, axis=0, keepdims=True)
        p = jnp.exp2(s - m)
        carries.append((m, jnp.sum(p, axis=0, keepdims=True),
                        _dot(vt_scr[h, i], p.astype(BF16))))

    def group(ns, carries):
        out = []
        for h in range(hps):
            m, l, acc = carries[h]
            ss = [jnp.where(sel_scr[h, pl.ds(n, 1), :] > 0.0, _nt_dot(kb_scr[h, n], qbs[h]), NEG_INF)
                  for n in ns]
            m_new = m
            for s in ss:
                m_new = jnp.maximum(m_new, jnp.max(s, axis=0, keepdims=True))
            alpha = jnp.exp2(m - m_new)
            l = alpha * l
            acc = alpha * acc
            for n, s in zip(ns, ss):
                p = jnp.exp2(s - m_new)
                l = l + jnp.sum(p, axis=0, keepdims=True)
                acc = acc + _dot(vt_scr[h, n], p.astype(BF16))
            out.append((m_new, l, acc))
        return tuple(out)

    carries = _grouped_sweep(i, group, tuple(carries), lambda v: v)
    for h in range(hps):
        _, l, acc = carries[h]
        o = (acc / l).T
        o_ref[:, _lanes(h)] = (o * _silu(g_ref[:, _lanes(h)])).astype(BF16)


def _moba_prompt(qa, ka, va, kmean, ga):
    b, t, wa = qa.shape
    hps = HEADS_PER_STEP
    w = hps * D_HEAD
    assert wa % w == 0
    blk = MOBA_BLOCK
    nq = t // blk
    nb = kmean.shape[1]
    kern = functools.partial(_moba_kernel, scale=D_HEAD ** -0.5)
    tile = pl.BlockSpec((None, blk, w), lambda bi, hi, i: (bi, i, hi))
    full = pl.BlockSpec((None, t, w), lambda bi, hi, i: (bi, 0, hi))
    return pl.pallas_call(
        kern,
        grid=(b, wa // w, nq),
        in_specs=[tile, full, full,
                  pl.BlockSpec((None, nb, w), lambda bi, hi, i: (bi, 0, hi)),
                  tile],
        out_specs=tile,
        out_shape=jax.ShapeDtypeStruct((b, t, wa), BF16),
        scratch_shapes=[pltpu.VMEM((hps, nq, blk, D_HEAD), BF16),
                        pltpu.VMEM((hps, nq, D_HEAD, blk), BF16),
                        pltpu.VMEM((hps, nb, blk), F32)],
        compiler_params=_cparams(("arbitrary",) * 3, 40),
    )(qa, ka, va, kmean, ga)


def _sb_kernel(q_ref, k_ref, v_ref, tri_ref, g_ref, o_ref, kb_scr, vt_scr, *, scale):
    i = pl.program_id(2)
    blk = q_ref.shape[0]
    hps = kb_scr.shape[0]

    @pl.when(i == 0)
    def _():
        _fill_kv(k_ref, v_ref, kb_scr, vt_scr)

    tri = tri_ref[...]
    keyi = lax.broadcasted_iota(jnp.int32, (blk, blk), 0)
    qi = lax.broadcasted_iota(jnp.int32, (blk, blk), 1)
    diag = keyi < qi

    def terms(h, n, mask):
        z = _nt_dot(kb_scr[h, n], q_ref[:, _lanes(h)]) * scale
        ls, lk = _log_sigmoid_pair(z)
        if mask is not None:
            lk = jnp.where(mask, lk, 0.0)
        hi, lo = _split2(lk)
        inner = _dot(tri, hi) + _dot(tri, lo)
        return ls, inner, inner[0:1, :] + lk[0:1, :]

    carries = []
    for h in range(hps):
        ls, inner, carry = terms(h, i, diag)
        w = jnp.where(diag, jnp.exp(ls + inner), 0.0)
        carries.append((carry, _dot(vt_scr[h, i], w.astype(BF16))))

    def group(ns, carries):
        out = []
        for h in range(hps):
            carry, acc = carries[h]
            parts = [terms(h, n, None) for n in ns]
            for n, (ls, inner, tot) in zip(ns, parts):
                w = jnp.exp(ls + (carry + inner))
                acc = acc + _dot(vt_scr[h, n], w.astype(BF16))
                carry = carry + tot
            out.append((carry, acc))
        return tuple(out)

    def alive(carries):
        top = carries[0][0]
        for h in range(1, hps):
            top = jnp.maximum(top, carries[h][0])
        return jnp.max(top) > SB_DEAD_LOG

    carries = _early_exit_sweep(i, group, tuple(carries), lambda v: i - 1 - v, alive)
    for h in range(hps):
        o_ref[:, _lanes(h)] = (carries[h][1].T * _silu(g_ref[:, _lanes(h)])).astype(BF16)


def _sb_prompt(qb_bf, kb, vb, gb, blk=256):
    b, t, wb = qb_bf.shape
    hps = HEADS_PER_STEP
    w = hps * D_HEAD
    assert wb % w == 0
    nq = t // blk
    r = jnp.arange(blk)
    tri = (r[None, :] > r[:, None]).astype(BF16)
    kern = functools.partial(_sb_kernel, scale=D_HEAD ** -0.5)
    tile = pl.BlockSpec((None, blk, w), lambda bi, hi, i: (bi, i, hi))
    full = pl.BlockSpec((None, t, w), lambda bi, hi, i: (bi, 0, hi))
    return pl.pallas_call(
        kern,
        grid=(b, wb // w, nq),
        in_specs=[tile, full, full,
                  pl.BlockSpec((blk, blk), lambda bi, hi, i: (0, 0)),
                  tile],
        out_specs=tile,
        out_shape=jax.ShapeDtypeStruct((b, t, wb), BF16),
        scratch_shapes=[pltpu.VMEM((hps, nq, blk, D_HEAD), BF16),
                        pltpu.VMEM((hps, nq, D_HEAD, blk), BF16)],
        compiler_params=_cparams(("arbitrary",) * 3, 40),
    )(qb_bf, kb, vb, tri, gb)


def _pool_kernel(u_ref, halo_ref, wp_ref, sc_ref, g_ref, o_ref, xp_ref, *, tq):
    i = pl.program_id(1)
    cg = D_HEAD
    halo = halo_ref[...]
    xp_ref[0:POOL_BUF + 1, :] = jnp.where(i > 0, halo, 0.0)
    xp_ref[POOL_BUF + 1:, :] = u_ref[...]
    pos = i * tq + lax.broadcasted_iota(jnp.int32, (tq, 1), 0)
    outs = []
    for gi, w in enumerate(POOL_WINDOWS):
        sl = slice(gi * cg, (gi + 1) * cg)
        x = xp_ref[POOL_BUF + 1:, sl]
        win = x
        for d in range(1, w):
            win = win + xp_ref[pl.ds(POOL_BUF + 1 - d, tq), sl]
        cnt = jnp.minimum(pos + 1, w).astype(F32)
        r = win / cnt - x
        outs.append(_dot(r.astype(BF16), wp_ref[gi].astype(BF16)))
    mixed = jnp.concatenate(outs, axis=1) * sc_ref[...]
    o_ref[...] = (mixed * _silu(g_ref[...])).astype(BF16)


def _pool_prompt(uc, gc, w_pool, pool_scale, layer, tq=512):
    b, t, wc = uc.shape
    hb = POOL_BUF + 1
    assert t % tq == 0 and tq % hb == 0
    halo_per_tile = tq // hb
    kern = functools.partial(_pool_kernel, tq=tq)
    tile = pl.BlockSpec((None, tq, wc), lambda bi, i: (bi, i, 0))
    return pl.pallas_call(
        kern,
        grid=(b, t // tq),
        in_specs=[tile,
                  pl.BlockSpec((None, hb, wc),
                               lambda bi, i: (bi, jnp.maximum(i * halo_per_tile - 1, 0), 0)),
                  pl.BlockSpec((None,) + w_pool.shape[1:], lambda bi, i: (layer, 0, 0, 0)),
                  pl.BlockSpec((None, 1, wc), lambda bi, i: (layer, 0, 0)),
                  tile],
        out_specs=tile,
        out_shape=jax.ShapeDtypeStruct((b, t, wc), BF16),
        scratch_shapes=[pltpu.VMEM((tq + hb, wc), F32)],
        compiler_params=_cparams(("arbitrary", "arbitrary"), 32),
    )(uc, uc, w_pool, pool_scale.reshape(pool_scale.shape[0], 1, wc), gc)


def _layer_norm(y, g, b):
    mu = jnp.mean(y, axis=-1, keepdims=True)
    d = y - mu
    var = jnp.mean(d * d, axis=-1, keepdims=True)
    return d * lax.rsqrt(var + LN_EPS) * g + b


def _merge_kernel(ma_ref, mb_ref, mc_ref, wa_ref, wb_ref, wc_ref, x_ref, g_ref, b_ref,
                  of_ref, ob_ref, *, alpha):
    y = (alpha * x_ref[...]
         + _dot(ma_ref[...], wa_ref[...])
         + _dot(mb_ref[...], wb_ref[...])
         + _dot(mc_ref[...], wc_ref[...]))
    out = _layer_norm(y, g_ref[...], b_ref[...])
    of_ref[...] = out
    ob_ref[...] = out.astype(BF16)


def _merge(mix_a, mix_b, mix_c, w_out_bf, x, ln_g, ln_b, layer, alpha, tm=256):
    m, d = x.shape
    wa, wb, wc = mix_a.shape[1], mix_b.shape[1], mix_c.shape[1]
    assert wb == wc and wa % wb == 0
    kern = functools.partial(_merge_kernel, alpha=alpha)
    vec = pl.BlockSpec((None, 1, d), lambda i: (layer, 0, 0))
    row = lambda w: pl.BlockSpec((tm, w), lambda i: (i, 0))
    return pl.pallas_call(
        kern,
        grid=(m // tm,),
        in_specs=[row(wa), row(wb), row(wc),
                  pl.BlockSpec((None, wa, d), lambda i: (layer, 0, 0)),
                  pl.BlockSpec((None, wb, d), lambda i: (layer, wa // wb, 0)),
                  pl.BlockSpec((None, wc, d), lambda i: (layer, wa // wb + 1, 0)),
                  row(d), vec, vec],
        out_specs=[row(d), row(d)],
        out_shape=[jax.ShapeDtypeStruct((m, d), F32), jax.ShapeDtypeStruct((m, d), BF16)],
        compiler_params=_cparams(("arbitrary",), 48),
    )(mix_a, mix_b, mix_c, w_out_bf, w_out_bf, w_out_bf, x,
      ln_g.reshape(ln_g.shape[0], 1, d), ln_b.reshape(ln_b.shape[0], 1, d))


def _proj_s_kernel(x_ref, w_ref, cos_ref, sin_ref, o_ref, *, rot_tiles):
    n = pl.program_id(0)
    acc = _dot(x_ref[...], w_ref[...])
    cos = cos_ref[...]
    sin = sin_ref[...]
    parts = []
    for j in range(acc.shape[1] // D_HEAD):
        xh = acc[:, j * D_HEAD:(j + 1) * D_HEAD]
        parts.append(xh * cos + pltpu.roll(xh, D_HEAD // 2, axis=1) * sin)
    rot = jnp.concatenate(parts, axis=1)
    o_ref[...] = jnp.where(n < rot_tiles, rot, acc)


def _proj_sample(xs_bf, w_bf, layer, cos_row, sin_row, rot_cols, tn=512):
    r, k = xs_bf.shape
    ncols = w_bf.shape[2]
    kern = functools.partial(_proj_s_kernel, rot_tiles=rot_cols // tn)
    return pl.pallas_call(
        kern,
        grid=(ncols // tn,),
        in_specs=[pl.BlockSpec((r, k), lambda n: (0, 0)),
                  pl.BlockSpec((None, k, tn), lambda n: (layer, 0, n)),
                  pl.BlockSpec((r, D_HEAD), lambda n: (0, 0)),
                  pl.BlockSpec((r, D_HEAD), lambda n: (0, 0))],
        out_specs=pl.BlockSpec((r, tn), lambda n: (0, n)),
        out_shape=jax.ShapeDtypeStruct((r, ncols), F32),
        compiler_params=_cparams(("arbitrary",), 32),
    )(xs_bf, w_bf, cos_row, sin_row)


def _page_rows(cache):
    depth, n_pool, page, heads, dh = cache.shape
    return cache.reshape(depth, n_pool, page * heads, dh)


def _head_rows(ref, h, page, heads):
    return ref[pl.ds(h, page, stride=heads), :]


def _stream_kernel(pt_ref, *refs, pps, n_steps, page, h_a, h_b, scale):
    del pt_ref
    ka_refs = refs[0:pps]
    kb_refs = refs[pps:2 * pps]
    vb_refs = refs[2 * pps:3 * pps]
    qa_ref, qb_ref, tri_ref = refs[3 * pps:3 * pps + 3]
    ids_ref, ob_ref = refs[3 * pps + 3:3 * pps + 5]
    ksum_ref, carry_ref, acc_ref = refs[3 * pps + 5:]
    s = pl.program_id(1)
    wb = h_b * D_HEAD
    nblk = ksum_ref.shape[0]

    @pl.when(s == 0)
    def _():
        ksum_ref[...] = jnp.zeros_like(ksum_ref)
        carry_ref[...] = jnp.zeros_like(carry_ref)
        acc_ref[...] = jnp.zeros_like(acc_ref)

    r = lax.broadcasted_iota(jnp.int32, (D_HEAD, wb), 0)
    c = lax.broadcasted_iota(jnp.int32, (D_HEAD, wb), 1)
    rtb = jnp.where(r == jnp.right_shift(c, 7),
                    jnp.broadcast_to(qb_ref[...], (D_HEAD, wb)), 0.0).astype(BF16)
    tri = tri_ref[...]

    last_page = n_steps * pps - 1
    for j in range(pps):
        p_idx = last_page - (s * pps + j)
        ka = ka_refs[j][...].reshape(page, h_a, D_HEAD)
        ksum_ref[_half(p_idx)] += jnp.sum(ka, axis=0)
        @pl.when(jnp.max(carry_ref[...]) > SB_DEAD_LOG)
        def _(j=j):
            kb = jnp.concatenate([_head_rows(kb_refs[j], h, page, h_b) for h in range(h_b)],
                                 axis=1).astype(BF16)
            z = _nt_dot(kb, rtb) * scale
            ls, lk = _log_sigmoid_pair(z)
            hi, lo = _split2(lk)
            inner = _dot(tri, hi) + _dot(tri, lo)
            carry = carry_ref[...]
            w = jnp.exp(ls + (carry + inner))
            carry_ref[...] = carry + (inner[0:1, :] + lk[0:1, :])
            for h in range(h_b):
                wv = w[:, h:h + 1] * _head_rows(vb_refs[j], h, page, h_b)
                acc_ref[h] += jnp.sum(wv.reshape(page // 8, 8, D_HEAD), axis=0)

    @pl.when(s == n_steps - 1)
    def _():
        for h in range(h_b):
            ob_ref[h:h + 1, :] = jnp.sum(acc_ref[h], axis=0, keepdims=True)
        q = qa_ref[...]
        lane = lax.broadcasted_iota(jnp.int32, (h_a, D_HEAD), 1)
        lanef = lane.astype(F32)
        gate = jnp.full((h_a, D_HEAD), NEG_INF, F32)
        for n in range(nblk):
            gn = jnp.sum(ksum_ref[n] * q, axis=1, keepdims=True) * (1.0 / MOBA_BLOCK)
            gate = jnp.where(lane == n, gn, gate)
        ids = jnp.zeros((h_a, D_HEAD), jnp.int32)
        for k in range(MOBA_TOPK):
            mx = jnp.max(gate, axis=1, keepdims=True)
            idx = jnp.min(jnp.where(gate == mx, lanef, float(D_HEAD)), axis=1, keepdims=True)
            ids = jnp.where(lane == k, idx.astype(jnp.int32), ids)
            gate = jnp.where(lanef == idx, NEG_INF, gate)
        ids_ref[...] = ids


def _stream_sample(page_table, cache_k_a, cache_k_b, cache_v_b, qa_s, qb_s, layer, pps=8):
    bd, n_pages = page_table.shape
    page, h_a = cache_k_a.shape[2], cache_k_a.shape[3]
    h_b = cache_k_b.shape[3]
    wa, wb = h_a * D_HEAD, h_b * D_HEAD
    pps = min(pps, n_pages)
    assert n_pages % pps == 0 and pps % 2 == 0 and (page * 2) == MOBA_BLOCK
    n_steps = n_pages // pps
    nblk = n_pages // 2
    assert MOBA_TOPK <= nblk <= D_HEAD
    r = jnp.arange(page)
    tri = (r[None, :] > r[:, None]).astype(BF16)

    def page_spec(heads, j):
        def imap(b, s, pt):
            return (layer, pt[b * n_pages + (n_pages - 1 - (s * pps + j))], 0, 0)
        return pl.BlockSpec((None, None, page * heads, D_HEAD), imap)

    in_specs = ([page_spec(h_a, j) for j in range(pps)]
                + [page_spec(h_b, j) for j in range(pps)]
                + [page_spec(h_b, j) for j in range(pps)]
                + [pl.BlockSpec((None, h_a, D_HEAD), lambda b, s, pt: (b, 0, 0)),
                   pl.BlockSpec((None, 1, wb), lambda b, s, pt: (b, 0, 0)),
                   pl.BlockSpec((page, page), lambda b, s, pt: (0, 0))])
    out_specs = [pl.BlockSpec((None, h_a, D_HEAD), lambda b, s, pt: (b, 0, 0)),
                 pl.BlockSpec((None, h_b, D_HEAD), lambda b, s, pt: (b, 0, 0))]
    kern = functools.partial(_stream_kernel, pps=pps, n_steps=n_steps, page=page,
                             h_a=h_a, h_b=h_b, scale=D_HEAD ** -0.5)
    ids, ob = pl.pallas_call(
        kern,
        grid_spec=pltpu.PrefetchScalarGridSpec(
            num_scalar_prefetch=1,
            grid=(bd, n_steps),
            in_specs=in_specs,
            out_specs=out_specs,
            scratch_shapes=[pltpu.VMEM((nblk, h_a, D_HEAD), F32),
                            pltpu.VMEM((1, D_HEAD), F32),
                            pltpu.VMEM((h_b, 8, D_HEAD), F32)]),
        out_shape=[jax.ShapeDtypeStruct((bd, h_a, D_HEAD), jnp.int32),
                   jax.ShapeDtypeStruct((bd, h_b, D_HEAD), F32)],
        compiler_params=_cparams(("arbitrary", "arbitrary"), 48),
    )(page_table.reshape(-1), *([_page_rows(cache_k_a)] * pps), *([_page_rows(cache_k_b)] * pps),
      *([_page_rows(cache_v_b)] * pps), qa_s.reshape(bd, h_a, D_HEAD), qb_s.reshape(bd, 1, wb), tri)
    return ids, ob


def _moba_s_kernel(pt_ref, ids_ref, *refs, n_pg, page, h_a, scale):
    del pt_ref, ids_ref
    k_refs = refs[0:n_pg]
    v_refs = refs[n_pg:2 * n_pg]
    q_ref, kn_ref, vn_ref, o_ref = refs[2 * n_pg:]
    h = pl.program_id(1)
    q = q_ref[...]
    q8 = jnp.broadcast_to(q, (8, D_HEAD)).astype(BF16)
    s_self = jnp.sum(q * kn_ref[...], axis=1, keepdims=True) * scale
    scores = [(_nt_dot(q8, _head_rows(k_refs[j], h, page, h_a).astype(BF16)) * scale)[0:1, :]
              for j in range(n_pg)]
    m = s_self
    for sc in scores:
        m = jnp.maximum(m, jnp.max(sc, axis=1, keepdims=True))
    p_self = jnp.exp(s_self - m)
    l = p_self
    acc = p_self * vn_ref[...]
    for j, sc in enumerate(scores):
        p = jnp.exp(sc - m)
        l = l + jnp.sum(p, axis=1, keepdims=True)
        p8 = jnp.broadcast_to(p, (8, p.shape[1])).astype(BF16)
        acc = acc + _dot(p8, _head_rows(v_refs[j], h, page, h_a).astype(BF16))[0:1, :]
    o_ref[...] = acc / l


def _moba_sample(page_table, ids, cache_k_a, cache_v_a, qa_s, ka_s, va_s, layer):
    bd, n_pages = page_table.shape
    page, h_a = cache_k_a.shape[2], cache_k_a.shape[3]
    wa = h_a * D_HEAD
    ppb = MOBA_BLOCK // page
    n_pg = MOBA_TOPK * ppb
    ids_flat = ids[:, :, :MOBA_TOPK].reshape(-1)

    def page_spec(j):
        kk, half = divmod(j, ppb)

        def imap(b, h, pt, idf):
            blk_id = idf[(b * h_a + h) * MOBA_TOPK + kk]
            return (layer, pt[b * n_pages + blk_id * ppb + half], 0, 0)
        return pl.BlockSpec((None, None, page * h_a, D_HEAD), imap)

    vec = pl.BlockSpec((None, None, 1, D_HEAD), lambda b, h, pt, idf: (b, h, 0, 0))
    kern = functools.partial(_moba_s_kernel, n_pg=n_pg, page=page, h_a=h_a, scale=D_HEAD ** -0.5)
    out = pl.pallas_call(
        kern,
        grid_spec=pltpu.PrefetchScalarGridSpec(
            num_scalar_prefetch=2,
            grid=(bd, h_a),
            in_specs=[page_spec(j) for j in range(n_pg)] * 2 + [vec, vec, vec],
            out_specs=vec),
        out_shape=jax.ShapeDtypeStruct((bd, h_a, 1, D_HEAD), F32),
        compiler_params=_cparams(("arbitrary", "arbitrary"), 32),
    )(page_table.reshape(-1), ids_flat, *([_page_rows(cache_k_a)] * n_pg),
      *([_page_rows(cache_v_a)] * n_pg),
      qa_s.reshape(bd, h_a, 1, D_HEAD), ka_s.reshape(bd, h_a, 1, D_HEAD),
      va_s.reshape(bd, h_a, 1, D_HEAD))
    return out.reshape(bd, wa)


def _post_s_kernel(oa_ref, ga_ref, ob_ref, gb_ref, u_ref, gc_ref, st_ref, wp_ref, sc_ref,
                   wa_ref, wb_ref, wc_ref, x_ref, g_ref, b_ref,
                   of_ref, obf_ref, ns_ref, *, alpha, pos):
    cg = D_HEAD
    u = u_ref[...]
    outs = []
    for gi, w in enumerate(POOL_WINDOWS):
        sl = slice(gi * cg, (gi + 1) * cg)
        win = u[:, sl]
        for d in range(1, w):
            win = win + st_ref[POOL_BUF - d][:, sl]
        r = win / float(min(pos + 1, w)) - u[:, sl]
        outs.append(_dot(r.astype(BF16), wp_ref[gi].astype(BF16)))
    oc = jnp.concatenate(outs, axis=1) * sc_ref[...]
    ma = (oa_ref[...] * _silu(ga_ref[...])).astype(BF16)
    mb = (ob_ref[...] * _silu(gb_ref[...])).astype(BF16)
    mc = (oc * _silu(gc_ref[...])).astype(BF16)
    y = (alpha * x_ref[...]
         + _dot(ma, wa_ref[...])
         + _dot(mb, wb_ref[...])
         + _dot(mc, wc_ref[...]))
    out = _layer_norm(y, g_ref[...], b_ref[...])
    of_ref[...] = out
    obf_ref[...] = out.astype(BF16)
    for d in range(POOL_BUF - 1):
        ns_ref[d] = st_ref[d + 1]
    ns_ref[POOL_BUF - 1] = u


def _post_sample(oa, ga, ob, gb, uc, gc, state_t, w_pool, pool_scale, w_out_bf, xs,
                 ln_g, ln_b, layer, alpha, pos):
    r, d = xs.shape
    wa, wb, wc = oa.shape[1], ob.shape[1], uc.shape[1]
    kern = functools.partial(_post_s_kernel, alpha=alpha, pos=pos)
    full = lambda a: pl.BlockSpec(a.shape, lambda i: (0,) * a.ndim)
    vec = pl.BlockSpec((None, 1, d), lambda i: (layer, 0, 0))
    return pl.pallas_call(
        kern,
        grid=(1,),
        in_specs=[full(oa), full(ga), full(ob), full(gb), full(uc), full(gc),
                  pl.BlockSpec((None,) + state_t.shape[1:], lambda i: (layer, 0, 0, 0)),
                  pl.BlockSpec((None,) + w_pool.shape[1:], lambda i: (layer, 0, 0, 0)),
                  pl.BlockSpec((None, 1, wc), lambda i: (layer, 0, 0)),
                  pl.BlockSpec((None, wa, d), lambda i: (layer, 0, 0)),
                  pl.BlockSpec((None, wb, d), lambda i: (layer, wa // wb, 0)),
                  pl.BlockSpec((None, wc, d), lambda i: (layer, wa // wb + 1, 0)),
                  full(xs), vec, vec],
        out_specs=[pl.BlockSpec((r, d), lambda i: (0, 0)), pl.BlockSpec((r, d), lambda i: (0, 0)),
                   pl.BlockSpec((POOL_BUF, r, wc), lambda i: (0, 0, 0))],
        out_shape=[jax.ShapeDtypeStruct((r, d), F32), jax.ShapeDtypeStruct((r, d), BF16),
                   jax.ShapeDtypeStruct((POOL_BUF, r, wc), F32)],
        compiler_params=_cparams(("arbitrary",), 48),
    )(oa, ga, ob, gb, uc, gc, state_t, w_pool,
      pool_scale.reshape(pool_scale.shape[0], 1, wc), w_out_bf, w_out_bf, w_out_bf, xs,
      ln_g.reshape(ln_g.shape[0], 1, d), ln_b.reshape(ln_b.shape[0], 1, d))


def _rope_tables(pos):
    half = D_HEAD // 2
    inv = 1.0 / (ROPE_THETA ** (jnp.arange(half, dtype=F32) * (2.0 / D_HEAD)))
    ang = pos.astype(F32)[:, None] * inv[None, :]
    cos, sin = jnp.cos(ang), jnp.sin(ang)
    return jnp.concatenate([cos, cos], axis=1), jnp.concatenate([-sin, sin], axis=1)


def kernel(x_prompt, x_sample, cache_k_a, cache_v_a, cache_k_b, cache_v_b, state_pool,
           page_table, w_in, w_pool, pool_scale, w_out, ln_g, ln_b):
    b, t, d = x_prompt.shape
    bd, ts, _ = x_sample.shape
    assert ts == 1
    depth = w_in.shape[0]
    h_a, h_b = cache_k_a.shape[3], cache_k_b.shape[3]
    wa, wb = h_a * D_HEAD, h_b * D_HEAD
    wc = state_pool.shape[3]
    assert w_in.shape[2] == 4 * wa + 4 * wb + 2 * wc
    n_pages = page_table.shape[1]
    past_len = n_pages * cache_k_a.shape[2]
    alpha = (2 * depth) ** 0.25
    m = b * t

    w_in_bf = w_in.astype(BF16)
    w_out_bf = w_out.astype(BF16)
    cos_p, sin_p = _rope_tables(jnp.arange(t, dtype=jnp.int32))
    cos_s, sin_s = _rope_tables(jnp.full((bd,), past_len, jnp.int32))
    state_t = state_pool.transpose(0, 2, 1, 3)

    c_qa, c_ka, c_va, c_ga = 0, wa, 2 * wa, 3 * wa
    c_qb = 4 * wa
    c_kb, c_vb, c_gb = c_qb + wb, c_qb + 2 * wb, c_qb + 3 * wb
    c_uc = c_qb + 4 * wb
    c_gc = c_uc + wc

    xp = x_prompt.reshape(m, d)
    xp_bf = xp.astype(BF16)
    xs = x_sample.reshape(bd, d)
    xs_bf = xs.astype(BF16)

    pka, pva, pkb, pvb, ppool = [], [], [], [], []
    ska, sva, skb, svb, spool = [], [], [], [], []
    for l in range(depth):
        proj = functools.partial(_proj, xp_bf, w_in_bf, l, seq_len=t)
        qa, ka, kmean = proj(c_qa, [(wa, F32, False), (wa, F32, True)], cos_t=cos_p, sin_t=sin_p)
        va, ga = proj(c_va, [(wa, F32, False)] * 2)
        qb_bf, kb, vb, gb = proj(c_qb, [(wb, BF16, False)] + [(wb, F32, False)] * 3)
        uc, gc = proj(c_uc, [(wc, F32, False)] * 2)

        r3 = lambda a: a.reshape(b, t, a.shape[-1])
        mix_a = _moba_prompt(r3(qa), r3(ka), r3(va), kmean.reshape(b, t // MOBA_BLOCK, wa), r3(ga))
        mix_b = _sb_prompt(r3(qb_bf), r3(kb), r3(vb), r3(gb))
        mix_c = _pool_prompt(r3(uc), r3(gc), w_pool, pool_scale, l)
        xp, xp_bf = _merge(mix_a.reshape(m, wa), mix_b.reshape(m, wb), mix_c.reshape(m, wc),
                           w_out_bf, xp, ln_g, ln_b, l, alpha)
        pka.append(ka.reshape(b, t, h_a, D_HEAD))
        pva.append(va.reshape(b, t, h_a, D_HEAD))
        pkb.append(kb.reshape(b, t, h_b, D_HEAD))
        pvb.append(vb.reshape(b, t, h_b, D_HEAD))
        ppool.append(r3(uc)[:, t - POOL_BUF:, :])

        hs = _proj_sample(xs_bf, w_in_bf, l, cos_s, sin_s, 2 * wa)
        qa_s, ka_s = hs[:, c_qa:c_qa + wa], hs[:, c_ka:c_ka + wa]
        va_s, ga_s = hs[:, c_va:c_va + wa], hs[:, c_ga:c_ga + wa]
        qb_s, kb_s = hs[:, c_qb:c_qb + wb], hs[:, c_kb:c_kb + wb]
        vb_s, gb_s = hs[:, c_vb:c_vb + wb], hs[:, c_gb:c_gb + wb]
        uc_s, gc_s = hs[:, c_uc:c_uc + wc], hs[:, c_gc:c_gc + wc]
        ids, ob_s = _stream_sample(page_table, cache_k_a, cache_k_b, cache_v_b, qa_s, qb_s, l)
        oa_s = _moba_sample(page_table, ids, cache_k_a, cache_v_a, qa_s, ka_s, va_s, l)
        xs, xs_bf, new_state = _post_sample(
            oa_s, ga_s, ob_s.reshape(bd, wb), gb_s, uc_s, gc_s, state_t, w_pool, pool_scale,
            w_out_bf, xs, ln_g, ln_b, l, alpha, past_len)
        ska.append(ka_s.reshape(bd, 1, h_a, D_HEAD))
        sva.append(va_s.reshape(bd, 1, h_a, D_HEAD))
        skb.append(kb_s.reshape(bd, 1, h_b, D_HEAD))
        svb.append(vb_s.reshape(bd, 1, h_b, D_HEAD))
        spool.append(new_state.transpose(1, 0, 2))

    return (xp.reshape(b, t, d), xs.reshape(bd, 1, d),
            jnp.stack(pka), jnp.stack(pva), jnp.stack(pkb), jnp.stack(pvb), jnp.stack(ppool),
            jnp.stack(ska), jnp.stack(sva), jnp.stack(skb), jnp.stack(svb), jnp.stack(spool))
```

```python
import functools

import jax
import jax.numpy as jnp
from jax import lax
from jax.experimental import pallas as pl
from jax.experimental.pallas import tpu as pltpu

F32 = jnp.float32
BF16 = jnp.bfloat16
NEG_INF = float("-inf")

D_HEAD = 128
MOBA_BLOCK = 256
MOBA_TOPK = 3
POOL_WINDOWS = (2, 4, 8, 16)
POOL_BUF = max(POOL_WINDOWS) - 1
ROPE_THETA = 10000.0
LN_EPS = 1e-5
LOG2E = 1.4426950408889634

_MIB = 1024 * 1024


def _cparams(semantics, vmem_mib):
    return pltpu.CompilerParams(dimension_semantics=semantics,
                                vmem_limit_bytes=vmem_mib * _MIB)


_NT = (((1,), (1,)), ((), ()))


def _nt_dot(a, b, precision=None):
    return lax.dot_general(a, b, _NT, precision=precision, preferred_element_type=F32)


def _dot(a, b):
    return jnp.dot(a, b, preferred_element_type=F32)


def _split2(x):
    hi = x.astype(BF16)
    lo = (x - hi.astype(F32)).astype(BF16)
    return hi, lo


def _log_sigmoid_pair(z):
    ls = jnp.minimum(z, 0.0) - jnp.log(1.0 + jnp.exp(-jnp.abs(z)))
    return ls, ls - z


def _silu(g):
    return g * jax.nn.sigmoid(g)


def _proj_kernel(*refs, rotary, segs):
    it = iter(refs)
    x_ref = next(it)
    w_ref = next(it)
    cos_ref = next(it) if rotary else None
    sin_ref = next(it) if rotary else None
    x = x_ref[...]
    c0 = 0
    for ncols, dtype, emit_mean in segs:
        acc = _dot(x, w_ref[:, c0:c0 + ncols])
        c0 += ncols
        if rotary:
            cos = cos_ref[...]
            sin = sin_ref[...]
            parts = []
            for j in range(ncols // D_HEAD):
                xh = acc[:, j * D_HEAD:(j + 1) * D_HEAD]
                parts.append(xh * cos + pltpu.roll(xh, D_HEAD // 2, axis=1) * sin)
            acc = jnp.concatenate(parts, axis=1)
        next(it)[...] = acc.astype(dtype)
        if emit_mean:
            tm = acc.shape[0]
            g = tm // MOBA_BLOCK
            next(it)[...] = jnp.sum(acc.reshape(g, MOBA_BLOCK, ncols), axis=1) * (1.0 / MOBA_BLOCK)


def _proj(x_bf, w_bf, layer, col0, segs, *, seq_len, cos_t=None, sin_t=None, tm=512):
    m, k = x_bf.shape
    total = sum(s[0] for s in segs)
    assert m % tm == 0 and col0 % total == 0 and seq_len % tm == 0 and tm % MOBA_BLOCK == 0
    rotary = cos_t is not None
    n_m = m // tm
    pos_tiles = seq_len // tm
    g = tm // MOBA_BLOCK

    in_specs = [pl.BlockSpec((tm, k), lambda i: (i, 0)),
                pl.BlockSpec((None, k, total), lambda i: (layer, 0, col0 // total))]
    args = [x_bf, w_bf]
    if rotary:
        in_specs += [pl.BlockSpec((tm, D_HEAD), lambda i: (i % pos_tiles, 0))] * 2
        args += [cos_t, sin_t]
    out_shape, out_specs = [], []
    for ncols, dtype, emit_mean in segs:
        out_shape.append(jax.ShapeDtypeStruct((m, ncols), dtype))
        out_specs.append(pl.BlockSpec((tm, ncols), lambda i: (i, 0)))
        if emit_mean:
            out_shape.append(jax.ShapeDtypeStruct((n_m, g, ncols), F32))
            out_specs.append(pl.BlockSpec((None, g, ncols), lambda i: (i, 0, 0)))
    kern = functools.partial(_proj_kernel, rotary=rotary, segs=tuple(segs))
    return pl.pallas_call(
        kern,
        grid=(n_m,),
        in_specs=in_specs,
        out_specs=out_specs,
        out_shape=out_shape,
        compiler_params=_cparams(("arbitrary",), 56),
    )(*args)


def _early_exit_sweep(count, group, carry, block_of, alive):
    def body(st):
        j, _, c = st
        c = group([block_of(j)], c)
        return j + 1, alive(c), c

    _, _, carry = lax.while_loop(lambda st: jnp.logical_and(st[0] < count, st[1]),
                                 body, (jnp.int32(0), alive(carry), carry))
    return carry


HEADS_PER_STEP = 2

SB_DEAD_LOG = -110.0


def _lanes(h):
    return slice(h * D_HEAD, (h + 1) * D_HEAD)


def _fill_kv(k_ref, v_ref, kb_scr, vt_scr):
    hps, nblk, blk, _ = kb_scr.shape
    for h in range(hps):
        for c in range(nblk):
            rows = slice(c * blk, (c + 1) * blk)
            kb_scr[h, c] = k_ref[rows, _lanes(h)].astype(BF16)
            vt_scr[h, c] = v_ref[rows, _lanes(h)].T.astype(BF16)


def _moba_kernel(q_ref, k_ref, v_ref, km_ref, g_ref, o_ref, kb_scr, vt_scr, sel_scr, *, scale):
    i = pl.program_id(2)
    blk = MOBA_BLOCK
    hps = kb_scr.shape[0]
    nb = km_ref.shape[0]

    @pl.when(i == 0)
    def _():
        _fill_kv(k_ref, v_ref, kb_scr, vt_scr)

    keyi = lax.broadcasted_iota(jnp.int32, (blk, blk), 0)
    qi = lax.broadcasted_iota(jnp.int32, (blk, blk), 1)
    rowi = lax.broadcasted_iota(jnp.int32, (nb, blk), 0)
    past = rowi < i

    qbs = []
    for h in range(hps):
        q = q_ref[:, _lanes(h)]
        gate = _nt_dot(km_ref[:, _lanes(h)], q, precision=lax.Precision.HIGHEST)
        gate = jnp.where(past, gate, NEG_INF)
        cnt = jnp.zeros(gate.shape, jnp.int32)
        for j in range(nb):
            gj = gate[j:j + 1, :]
            cnt = cnt + jnp.where(gj > gate, 1,
                                  jnp.where(gj == gate, jnp.where(rowi > j, 1, 0), 0))
        sel_scr[h] = jnp.where(past, jnp.where(cnt < MOBA_TOPK, 1.0, 0.0), 0.0)
        qbs.append((q * (scale * LOG2E)).astype(BF16))

    def scores(h, n):
        s = _nt_dot(kb_scr[h, n], qbs[h])
        if n is i:
            return jnp.where(keyi <= qi, s, NEG_INF)
        return jnp.where(sel_scr[h, pl.ds(n, 1), :] > 0.0, s, NEG_INF)

    def group(ns, carries):
        out = []
        for h in range(hps):
            ss = [scores(h, n) for n in ns]
            m_new = None if carries is None else carries[h][0]
            for s in ss:
                mx = jnp.max(s, axis=0, keepdims=True)
                m_new = mx if m_new is None else jnp.maximum(m_new, mx)
            if carries is None:
                l = jnp.zeros_like(m_new)
                acc = jnp.zeros((D_HEAD, blk), F32)
            else:
                alpha = jnp.exp2(carries[h][0] - m_new)
                l = alpha * carries[h][1]
                acc = alpha * carries[h][2]
            for n, s in zip(ns, ss):
                p = jnp.exp2(s - m_new)
                l = l + jnp.sum(p, axis=0, keepdims=True)
                acc = acc + _dot(vt_scr[h, n], p.astype(BF16))
            out.append((m_new, l, acc))
        return tuple(out)

    quads = lax.shift_right_logical(i, 2)
    first = [lambda _, r=r: group([i] + [4 * quads + k for k in range(r)], None) for r in range(4)]
    carries = lax.switch(jnp.bitwise_and(i, 3), first, 0)
    carries = lax.fori_loop(0, quads, lambda j, c: group([4 * j + k for k in range(4)], c), carries)
    for h in range(hps):
        _, l, acc = carries[h]
        o = (acc / l).T
        o_ref[:, _lanes(h)] = (o * _silu(g_ref[:, _lanes(h)])).astype(BF16)


def _moba_prompt(qa, ka, va, kmean, ga, hps=4):
    b, t, wa = qa.shape
    w = hps * D_HEAD
    assert wa % w == 0
    blk = MOBA_BLOCK
    nq = t // blk
    nb = kmean.shape[1]
    kern = functools.partial(_moba_kernel, scale=D_HEAD ** -0.5)
    tile = pl.BlockSpec((None, blk, w), lambda bi, hi, i: (bi, i, hi))
    full = pl.BlockSpec((None, t, w), lambda bi, hi, i: (bi, 0, hi))
    return pl.pallas_call(
        kern,
        grid=(b, wa // w, nq),
        in_specs=[tile, full, full,
                  pl.BlockSpec((None, nb, w), lambda bi, hi, i: (bi, 0, hi)),
                  tile],
        out_specs=tile,
        out_shape=jax.ShapeDtypeStruct((b, t, wa), BF16),
        scratch_shapes=[pltpu.VMEM((hps, nq, blk, D_HEAD), BF16),
                        pltpu.VMEM((hps, nq, D_HEAD, blk), BF16),
                        pltpu.VMEM((hps, nb, blk), F32)],
        compiler_params=_cparams(("arbitrary",) * 3, 58),
    )(qa, ka, va, kmean, ga)


def _sb_kernel(q_ref, k_ref, v_ref, tri_ref, g_ref, o_ref, kb_scr, vt_scr, *, scale):
    i = pl.program_id(2)
    blk = q_ref.shape[0]
    hps = kb_scr.shape[0]

    @pl.when(i == 0)
    def _():
        _fill_kv(k_ref, v_ref, kb_scr, vt_scr)

    tri = tri_ref[...]
    keyi = lax.broadcasted_iota(jnp.int32, (blk, blk), 0)
    qi = lax.broadcasted_iota(jnp.int32, (blk, blk), 1)
    diag = keyi < qi

    def terms(h, n, mask):
        z = _nt_dot(kb_scr[h, n], q_ref[:, _lanes(h)]) * scale
        ls, lk = _log_sigmoid_pair(z)
        if mask is not None:
            lk = jnp.where(mask, lk, 0.0)
        hi, lo = _split2(lk)
        inner = _dot(tri, hi) + _dot(tri, lo)
        return ls, inner, inner[0:1, :] + lk[0:1, :]

    carries = []
    for h in range(hps):
        ls, inner, carry = terms(h, i, diag)
        w = jnp.where(diag, jnp.exp(ls + inner), 0.0)
        carries.append((carry, _dot(vt_scr[h, i], w.astype(BF16))))

    def group(ns, carries):
        out = []
        for h in range(hps):
            carry, acc = carries[h]
            parts = [terms(h, n, None) for n in ns]
            for n, (ls, inner, tot) in zip(ns, parts):
                w = jnp.exp(ls + (carry + inner))
                acc = acc + _dot(vt_scr[h, n], w.astype(BF16))
                carry = carry + tot
            out.append((carry, acc))
        return tuple(out)

    def alive(carries):
        top = carries[0][0]
        for h in range(1, hps):
            top = jnp.maximum(top, carries[h][0])
        return jnp.max(top) > SB_DEAD_LOG

    carries = _early_exit_sweep(i, group, tuple(carries), lambda v: i - 1 - v, alive)
    for h in range(hps):
        o_ref[:, _lanes(h)] = (carries[h][1].T * _silu(g_ref[:, _lanes(h)])).astype(BF16)


def _sb_prompt(qb_bf, kb, vb, gb, blk=256):
    b, t, wb = qb_bf.shape
    hps = HEADS_PER_STEP
    w = hps * D_HEAD
    assert wb % w == 0
    nq = t // blk
    r = jnp.arange(blk)
    tri = (r[None, :] > r[:, None]).astype(BF16)
    kern = functools.partial(_sb_kernel, scale=D_HEAD ** -0.5)
    tile = pl.BlockSpec((None, blk, w), lambda bi, hi, i: (bi, i, hi))
    full = pl.BlockSpec((None, t, w), lambda bi, hi, i: (bi, 0, hi))
    return pl.pallas_call(
        kern,
        grid=(b, wb // w, nq),
        in_specs=[tile, full, full,
                  pl.BlockSpec((blk, blk), lambda bi, hi, i: (0, 0)),
                  tile],
        out_specs=tile,
        out_shape=jax.ShapeDtypeStruct((b, t, wb), BF16),
        scratch_shapes=[pltpu.VMEM((hps, nq, blk, D_HEAD), BF16),
                        pltpu.VMEM((hps, nq, D_HEAD, blk), BF16)],
        compiler_params=_cparams(("arbitrary",) * 3, 40),
    )(qb_bf, kb, vb, tri, gb)


def _pool_kernel(u_ref, halo_ref, wp_ref, sc_ref, g_ref, o_ref, xp_ref, *, tq):
    i = pl.program_id(1)
    cg = D_HEAD
    halo = halo_ref[...]
    xp_ref[0:POOL_BUF + 1, :] = jnp.where(i > 0, halo, 0.0)
    xp_ref[POOL_BUF + 1:, :] = u_ref[...]
    pos = i * tq + lax.broadcasted_iota(jnp.int32, (tq, 1), 0)
    outs = []
    for gi, w in enumerate(POOL_WINDOWS):
        sl = slice(gi * cg, (gi + 1) * cg)
        x = xp_ref[POOL_BUF + 1:, sl]
        win = x
        for d in range(1, w):
            win = win + xp_ref[pl.ds(POOL_BUF + 1 - d, tq), sl]
        cnt = jnp.minimum(pos + 1, w).astype(F32)
        r = win / cnt - x
        outs.append(_dot(r.astype(BF16), wp_ref[gi].astype(BF16)))
    mixed = jnp.concatenate(outs, axis=1) * sc_ref[...]
    o_ref[...] = (mixed * _silu(g_ref[...])).astype(BF16)


def _pool_prompt(uc, gc, w_pool, pool_scale, layer, tq=512):
    b, t, wc = uc.shape
    hb = POOL_BUF + 1
    assert t % tq == 0 and tq % hb == 0
    halo_per_tile = tq // hb
    kern = functools.partial(_pool_kernel, tq=tq)
    tile = pl.BlockSpec((None, tq, wc), lambda bi, i: (bi, i, 0))
    return pl.pallas_call(
        kern,
        grid=(b, t // tq),
        in_specs=[tile,
                  pl.BlockSpec((None, hb, wc),
                               lambda bi, i: (bi, jnp.maximum(i * halo_per_tile - 1, 0), 0)),
                  pl.BlockSpec((None,) + w_pool.shape[1:], lambda bi, i: (layer, 0, 0, 0)),
                  pl.BlockSpec((None, 1, wc), lambda bi, i: (layer, 0, 0)),
                  tile],
        out_specs=tile,
        out_shape=jax.ShapeDtypeStruct((b, t, wc), BF16),
        scratch_shapes=[pltpu.VMEM((tq + hb, wc), F32)],
        compiler_params=_cparams(("arbitrary", "arbitrary"), 32),
    )(uc, uc, w_pool, pool_scale.reshape(pool_scale.shape[0], 1, wc), gc)


def _layer_norm(y, g, b):
    mu = jnp.mean(y, axis=-1, keepdims=True)
    d = y - mu
    var = jnp.mean(d * d, axis=-1, keepdims=True)
    return d * lax.rsqrt(var + LN_EPS) * g + b


def _merge_kernel(ma_ref, mb_ref, mc_ref, wa_ref, wb_ref, wc_ref, x_ref, g_ref, b_ref,
                  of_ref, ob_ref, *, alpha):
    y = (alpha * x_ref[...]
         + _dot(ma_ref[...], wa_ref[...])
         + _dot(mb_ref[...], wb_ref[...])
         + _dot(mc_ref[...], wc_ref[...]))
    out = _layer_norm(y, g_ref[...], b_ref[...])
    of_ref[...] = out
    ob_ref[...] = out.astype(BF16)


def _merge(mix_a, mix_b, mix_c, w_out_bf, x, ln_g, ln_b, layer, alpha, tm=512):
    m, d = x.shape
    wa, wb, wc = mix_a.shape[1], mix_b.shape[1], mix_c.shape[1]
    assert wb == wc and wa % wb == 0
    kern = functools.partial(_merge_kernel, alpha=alpha)
    vec = pl.BlockSpec((None, 1, d), lambda i: (layer, 0, 0))
    row = lambda w: pl.BlockSpec((tm, w), lambda i: (i, 0))
    return pl.pallas_call(
        kern,
        grid=(m // tm,),
        in_specs=[row(wa), row(wb), row(wc),
                  pl.BlockSpec((None, wa, d), lambda i: (layer, 0, 0)),
                  pl.BlockSpec((None, wb, d), lambda i: (layer, wa // wb, 0)),
                  pl.BlockSpec((None, wc, d), lambda i: (layer, wa // wb + 1, 0)),
                  row(d), vec, vec],
        out_specs=[row(d), row(d)],
        out_shape=[jax.ShapeDtypeStruct((m, d), F32), jax.ShapeDtypeStruct((m, d), BF16)],
        compiler_params=_cparams(("arbitrary",), 56),
    )(mix_a, mix_b, mix_c, w_out_bf, w_out_bf, w_out_bf, x,
      ln_g.reshape(ln_g.shape[0], 1, d), ln_b.reshape(ln_b.shape[0], 1, d))


def _proj_s_kernel(x_ref, w_ref, cos_ref, sin_ref, o_ref, *, rot_tiles):
    n = pl.program_id(0)
    acc = _dot(x_ref[...], w_ref[...])
    cos = cos_ref[...]
    sin = sin_ref[...]
    parts = []
    for j in range(acc.shape[1] // D_HEAD):
        xh = acc[:, j * D_HEAD:(j + 1) * D_HEAD]
        parts.append(xh * cos + pltpu.roll(xh, D_HEAD // 2, axis=1) * sin)
    rot = jnp.concatenate(parts, axis=1)
    o_ref[...] = jnp.where(n < rot_tiles, rot, acc)


def _proj_sample(xs_bf, w_bf, layer, cos_row, sin_row, rot_cols, tn=512):
    r, k = xs_bf.shape
    ncols = w_bf.shape[2]
    kern = functools.partial(_proj_s_kernel, rot_tiles=rot_cols // tn)
    return pl.pallas_call(
        kern,
        grid=(ncols // tn,),
        in_specs=[pl.BlockSpec((r, k), lambda n: (0, 0)),
                  pl.BlockSpec((None, k, tn), lambda n: (layer, 0, n)),
                  pl.BlockSpec((r, D_HEAD), lambda n: (0, 0)),
                  pl.BlockSpec((r, D_HEAD), lambda n: (0, 0))],
        out_specs=pl.BlockSpec((r, tn), lambda n: (0, n)),
        out_shape=jax.ShapeDtypeStruct((r, ncols), F32),
        compiler_params=_cparams(("arbitrary",), 32),
    )(xs_bf, w_bf, cos_row, sin_row)


def _page_rows(cache):
    depth, n_pool, page, heads, dh = cache.shape
    return cache.reshape(depth, n_pool, page * heads, dh)


def _head_rows(ref, h, page, heads):
    return ref[pl.ds(h, page, stride=heads), :]


def _gate_kernel(pt_ref, *refs, pps, n_steps, page, h_a):
    del pt_ref
    ka_refs = refs[0:pps]
    qa_ref, ids_ref, ksum_ref = refs[pps:]
    s = pl.program_id(1)
    nblk = ksum_ref.shape[0]

    for j in range(0, pps, 2):
        ksum_ref[s * (pps // 2) + j // 2] = (
            jnp.sum(ka_refs[j][...].reshape(page, h_a, D_HEAD), axis=0)
            + jnp.sum(ka_refs[j + 1][...].reshape(page, h_a, D_HEAD), axis=0))

    @pl.when(s == n_steps - 1)
    def _():
        q = qa_ref[...]
        lane = lax.broadcasted_iota(jnp.int32, (h_a, D_HEAD), 1)
        lanef = lane.astype(F32)
        gate = jnp.full((h_a, D_HEAD), NEG_INF, F32)
        for n in range(nblk):
            gn = jnp.sum(ksum_ref[n] * q, axis=1, keepdims=True) * (1.0 / MOBA_BLOCK)
            gate = jnp.where(lane == n, gn, gate)
        ids = jnp.zeros((h_a, D_HEAD), jnp.int32)
        for k in range(MOBA_TOPK):
            mx = jnp.max(gate, axis=1, keepdims=True)
            idx = jnp.min(jnp.where(gate == mx, lanef, float(D_HEAD)), axis=1, keepdims=True)
            ids = jnp.where(lane == k, idx.astype(jnp.int32), ids)
            gate = jnp.where(lanef == idx, NEG_INF, gate)
        ids_ref[...] = ids


def _gate_sample(page_table, cache_k_a, qa_s, layer, pps=16):
    bd, n_pages = page_table.shape
    page, h_a = cache_k_a.shape[2], cache_k_a.shape[3]
    pps = min(pps, n_pages)
    assert n_pages % pps == 0 and pps % 2 == 0 and (page * 2) == MOBA_BLOCK
    n_steps = n_pages // pps
    nblk = n_pages // 2
    assert MOBA_TOPK <= nblk <= D_HEAD

    def page_spec(j):
        return pl.BlockSpec((None, None, page * h_a, D_HEAD),
                            lambda b, s, pt: (layer, pt[b * n_pages + s * pps + j], 0, 0))

    kern = functools.partial(_gate_kernel, pps=pps, n_steps=n_steps, page=page, h_a=h_a)
    return pl.pallas_call(
        kern,
        grid_spec=pltpu.PrefetchScalarGridSpec(
            num_scalar_prefetch=1,
            grid=(bd, n_steps),
            in_specs=([page_spec(j) for j in range(pps)]
                      + [pl.BlockSpec((None, h_a, D_HEAD), lambda b, s, pt: (b, 0, 0))]),
            out_specs=pl.BlockSpec((None, h_a, D_HEAD), lambda b, s, pt: (b, 0, 0)),
            scratch_shapes=[pltpu.VMEM((nblk, h_a, D_HEAD), F32)]),
        out_shape=jax.ShapeDtypeStruct((bd, h_a, D_HEAD), jnp.int32),
        compiler_params=_cparams(("arbitrary", "arbitrary"), 40),
    )(page_table.reshape(-1), *([_page_rows(cache_k_a)] * pps), qa_s.reshape(bd, h_a, D_HEAD))


def _sb_s_kernel(pt_ref, kb_hbm, vb_hbm, qb_ref, tri_ref, ob_ref, kbuf, vbuf, sem, carry_ref, acc_ref,
                 *, layer, n_pages, cpp, page, h_b, scale):
    b = pl.program_id(0)
    n_chunks = n_pages // cpp
    wb = h_b * D_HEAD

    def copies(chunk, slot):
        out = []
        for j in range(cpp):
            pg = pt_ref[b * n_pages + (n_pages - 1 - (chunk * cpp + j))]
            out.append(pltpu.make_async_copy(kb_hbm.at[layer, pg], kbuf.at[slot, j], sem.at[slot, 0, j]))
            out.append(pltpu.make_async_copy(vb_hbm.at[layer, pg], vbuf.at[slot, j], sem.at[slot, 1, j]))
        return out

    def start(chunk, slot):
        for cp in copies(chunk, slot):
            cp.start()

    def wait(chunk, slot):
        for cp in copies(chunk, slot):
            cp.wait()

    carry_ref[...] = jnp.zeros_like(carry_ref)
    acc_ref[...] = jnp.zeros_like(acc_ref)
    r = lax.broadcasted_iota(jnp.int32, (D_HEAD, wb), 0)
    c = lax.broadcasted_iota(jnp.int32, (D_HEAD, wb), 1)
    rtb = jnp.where(r == jnp.right_shift(c, 7),
                    jnp.broadcast_to(qb_ref[...], (D_HEAD, wb)), 0.0).astype(BF16)
    tri = tri_ref[...]

    def alive():
        return jnp.max(carry_ref[...]) > SB_DEAD_LOG

    start(0, 0)

    def body(st):
        chunk, _ = st
        slot = jnp.bitwise_and(chunk, 1)

        @pl.when(chunk + 1 < n_chunks)
        def _():
            start(chunk + 1, 1 - slot)

        wait(chunk, slot)
        for j in range(cpp):
            @pl.when(alive())
            def _(j=j):
                kref = kbuf.at[slot, j]
                vref = vbuf.at[slot, j]
                kb = jnp.concatenate([_head_rows(kref, h, page, h_b) for h in range(h_b)],
                                     axis=1).astype(BF16)
                z = _nt_dot(kb, rtb) * scale
                ls, lk = _log_sigmoid_pair(z)
                hi, lo = _split2(lk)
                inner = _dot(tri, hi) + _dot(tri, lo)
                carry = carry_ref[...]
                w = jnp.exp(ls + (carry + inner))
                carry_ref[...] = carry + (inner[0:1, :] + lk[0:1, :])
                for h in range(h_b):
                    wv = w[:, h:h + 1] * _head_rows(vref, h, page, h_b)
                    acc_ref[h] += jnp.sum(wv.reshape(page // 8, 8, D_HEAD), axis=0)
        return chunk + 1, alive()

    done, _ = lax.while_loop(lambda st: jnp.logical_and(st[0] < n_chunks, st[1]),
                             body, (jnp.int32(0), jnp.bool_(True)))

    @pl.when(done < n_chunks)
    def _():
        wait(done, jnp.bitwise_and(done, 1))

    for h in range(h_b):
        ob_ref[h:h + 1, :] = jnp.sum(acc_ref[h], axis=0, keepdims=True)


def _sb_sample(page_table, cache_k_b, cache_v_b, qb_s, layer, cpp=4):
    bd, n_pages = page_table.shape
    page, h_b = cache_k_b.shape[2], cache_k_b.shape[3]
    wb = h_b * D_HEAD
    cpp = min(cpp, n_pages)
    assert n_pages % cpp == 0 and n_pages >= cpp >= 1
    rows = page * h_b
    r = jnp.arange(page)
    tri = (r[None, :] > r[:, None]).astype(BF16)
    kern = functools.partial(_sb_s_kernel, layer=layer, n_pages=n_pages, cpp=cpp, page=page,
                             h_b=h_b, scale=D_HEAD ** -0.5)
    return pl.pallas_call(
        kern,
        grid_spec=pltpu.PrefetchScalarGridSpec(
            num_scalar_prefetch=1,
            grid=(bd,),
            in_specs=[pl.BlockSpec(memory_space=pl.ANY),
                      pl.BlockSpec(memory_space=pl.ANY),
                      pl.BlockSpec((None, 1, wb), lambda b, pt: (b, 0, 0)),
                      pl.BlockSpec((page, page), lambda b, pt: (0, 0))],
            out_specs=pl.BlockSpec((None, h_b, D_HEAD), lambda b, pt: (b, 0, 0)),
            scratch_shapes=[pltpu.VMEM((2, cpp, rows, D_HEAD), F32),
                            pltpu.VMEM((2, cpp, rows, D_HEAD), F32),
                            pltpu.SemaphoreType.DMA((2, 2, cpp)),
                            pltpu.VMEM((1, D_HEAD), F32),
                            pltpu.VMEM((h_b, 8, D_HEAD), F32)]),
        out_shape=jax.ShapeDtypeStruct((bd, h_b, D_HEAD), F32),
        compiler_params=_cparams(("arbitrary",), 32),
    )(page_table.reshape(-1), _page_rows(cache_k_b), _page_rows(cache_v_b),
      qb_s.reshape(bd, 1, wb), tri)


def _moba_s_kernel(pt_ref, ids_ref, *refs, n_pg, page, h_a, scale):
    del pt_ref, ids_ref
    k_refs = refs[0:n_pg]
    v_refs = refs[n_pg:2 * n_pg]
    q_ref, kn_ref, vn_ref, o_ref = refs[2 * n_pg:]
    h = pl.program_id(1)
    q = q_ref[...]
    q8 = jnp.broadcast_to(q, (8, D_HEAD)).astype(BF16)
    s_self = jnp.sum(q * kn_ref[...], axis=1, keepdims=True) * scale
    scores = [(_nt_dot(q8, _head_rows(k_refs[j], h, page, h_a).astype(BF16)) * scale)[0:1, :]
              for j in range(n_pg)]
    m = s_self
    for sc in scores:
        m = jnp.maximum(m, jnp.max(sc, axis=1, keepdims=True))
    p_self = jnp.exp(s_self - m)
    l = p_self
    acc = p_self * vn_ref[...]
    for j, sc in enumerate(scores):
        p = jnp.exp(sc - m)
        l = l + jnp.sum(p, axis=1, keepdims=True)
        p8 = jnp.broadcast_to(p, (8, p.shape[1])).astype(BF16)
        acc = acc + _dot(p8, _head_rows(v_refs[j], h, page, h_a).astype(BF16))[0:1, :]
    o_ref[...] = acc / l


def _moba_sample(page_table, ids, cache_k_a, cache_v_a, qa_s, ka_s, va_s, layer):
    bd, n_pages = page_table.shape
    page, h_a = cache_k_a.shape[2], cache_k_a.shape[3]
    wa = h_a * D_HEAD
    ppb = MOBA_BLOCK // page
    n_pg = MOBA_TOPK * ppb
    ids_flat = ids[:, :, :MOBA_TOPK].reshape(-1)

    def page_spec(j):
        kk, half = divmod(j, ppb)

        def imap(b, h, pt, idf):
            blk_id = idf[(b * h_a + h) * MOBA_TOPK + kk]
            return (layer, pt[b * n_pages + blk_id * ppb + half], 0, 0)
        return pl.BlockSpec((None, None, page * h_a, D_HEAD), imap)

    vec = pl.BlockSpec((None, None, 1, D_HEAD), lambda b, h, pt, idf: (b, h, 0, 0))
    kern = functools.partial(_moba_s_kernel, n_pg=n_pg, page=page, h_a=h_a, scale=D_HEAD ** -0.5)
    out = pl.pallas_call(
        kern,
        grid_spec=pltpu.PrefetchScalarGridSpec(
            num_scalar_prefetch=2,
            grid=(bd, h_a),
            in_specs=[page_spec(j) for j in range(n_pg)] * 2 + [vec, vec, vec],
            out_specs=vec),
        out_shape=jax.ShapeDtypeStruct((bd, h_a, 1, D_HEAD), F32),
        compiler_params=_cparams(("arbitrary", "arbitrary"), 32),
    )(page_table.reshape(-1), ids_flat, *([_page_rows(cache_k_a)] * n_pg),
      *([_page_rows(cache_v_a)] * n_pg),
      qa_s.reshape(bd, h_a, 1, D_HEAD), ka_s.reshape(bd, h_a, 1, D_HEAD),
      va_s.reshape(bd, h_a, 1, D_HEAD))
    return out.reshape(bd, wa)


def _post_s_kernel(oa_ref, ga_ref, ob_ref, gb_ref, u_ref, gc_ref, st_ref, wp_ref, sc_ref,
                   wa_ref, wb_ref, wc_ref, x_ref, g_ref, b_ref,
                   of_ref, obf_ref, ns_ref, *, alpha, pos):
    cg = D_HEAD
    u = u_ref[...]
    outs = []
    for gi, w in enumerate(POOL_WINDOWS):
        sl = slice(gi * cg, (gi + 1) * cg)
        win = u[:, sl]
        for d in range(1, w):
            win = win + st_ref[POOL_BUF - d][:, sl]
        r = win / float(min(pos + 1, w)) - u[:, sl]
        outs.append(_dot(r.astype(BF16), wp_ref[gi].astype(BF16)))
    oc = jnp.concatenate(outs, axis=1) * sc_ref[...]
    ma = (oa_ref[...] * _silu(ga_ref[...])).astype(BF16)
    mb = (ob_ref[...] * _silu(gb_ref[...])).astype(BF16)
    mc = (oc * _silu(gc_ref[...])).astype(BF16)
    y = (alpha * x_ref[...]
         + _dot(ma, wa_ref[...])
         + _dot(mb, wb_ref[...])
         + _dot(mc, wc_ref[...]))
    out = _layer_norm(y, g_ref[...], b_ref[...])
    of_ref[...] = out
    obf_ref[...] = out.astype(BF16)
    for d in range(POOL_BUF - 1):
        ns_ref[d] = st_ref[d + 1]
    ns_ref[POOL_BUF - 1] = u


def _post_sample(oa, ga, ob, gb, uc, gc, state_t, w_pool, pool_scale, w_out_bf, xs,
                 ln_g, ln_b, layer, alpha, pos):
    r, d = xs.shape
    wa, wb, wc = oa.shape[1], ob.shape[1], uc.shape[1]
    kern = functools.partial(_post_s_kernel, alpha=alpha, pos=pos)
    full = lambda a: pl.BlockSpec(a.shape, lambda i: (0,) * a.ndim)
    vec = pl.BlockSpec((None, 1, d), lambda i: (layer, 0, 0))
    return pl.pallas_call(
        kern,
        grid=(1,),
        in_specs=[full(oa), full(ga), full(ob), full(gb), full(uc), full(gc),
                  pl.BlockSpec((None,) + state_t.shape[1:], lambda i: (layer, 0, 0, 0)),
                  pl.BlockSpec((None,) + w_pool.shape[1:], lambda i: (layer, 0, 0, 0)),
                  pl.BlockSpec((None, 1, wc), lambda i: (layer, 0, 0)),
                  pl.BlockSpec((None, wa, d), lambda i: (layer, 0, 0)),
                  pl.BlockSpec((None, wb, d), lambda i: (layer, wa // wb, 0)),
                  pl.BlockSpec((None, wc, d), lambda i: (layer, wa // wb + 1, 0)),
                  full(xs), vec, vec],
        out_specs=[pl.BlockSpec((r, d), lambda i: (0, 0)), pl.BlockSpec((r, d), lambda i: (0, 0)),
                   pl.BlockSpec((POOL_BUF, r, wc), lambda i: (0, 0, 0))],
        out_shape=[jax.ShapeDtypeStruct((r, d), F32), jax.ShapeDtypeStruct((r, d), BF16),
                   jax.ShapeDtypeStruct((POOL_BUF, r, wc), F32)],
        compiler_params=_cparams(("arbitrary",), 48),
    )(oa, ga, ob, gb, uc, gc, state_t, w_pool,
      pool_scale.reshape(pool_scale.shape[0], 1, wc), w_out_bf, w_out_bf, w_out_bf, xs,
      ln_g.reshape(ln_g.shape[0], 1, d), ln_b.reshape(ln_b.shape[0], 1, d))


def _rope_tables(pos):
    half = D_HEAD // 2
    inv = 1.0 / (ROPE_THETA ** (jnp.arange(half, dtype=F32) * (2.0 / D_HEAD)))
    ang = pos.astype(F32)[:, None] * inv[None, :]
    cos, sin = jnp.cos(ang), jnp.sin(ang)
    return jnp.concatenate([cos, cos], axis=1), jnp.concatenate([-sin, sin], axis=1)


def kernel(x_prompt, x_sample, cache_k_a, cache_v_a, cache_k_b, cache_v_b, state_pool,
           page_table, w_in, w_pool, pool_scale, w_out, ln_g, ln_b):
    b, t, d = x_prompt.shape
    bd, ts, _ = x_sample.shape
    assert ts == 1
    depth = w_in.shape[0]
    h_a, h_b = cache_k_a.shape[3], cache_k_b.shape[3]
    wa, wb = h_a * D_HEAD, h_b * D_HEAD
    wc = state_pool.shape[3]
    assert w_in.shape[2] == 4 * wa + 4 * wb + 2 * wc
    n_pages = page_table.shape[1]
    past_len = n_pages * cache_k_a.shape[2]
    alpha = (2 * depth) ** 0.25
    m = b * t

    w_in_bf = w_in.astype(BF16)
    w_out_bf = w_out.astype(BF16)
    cos_p, sin_p = _rope_tables(jnp.arange(t, dtype=jnp.int32))
    cos_s, sin_s = _rope_tables(jnp.full((bd,), past_len, jnp.int32))
    state_t = state_pool.transpose(0, 2, 1, 3)

    c_qa, c_ka, c_va, c_ga = 0, wa, 2 * wa, 3 * wa
    c_qb = 4 * wa
    c_kb, c_vb, c_gb = c_qb + wb, c_qb + 2 * wb, c_qb + 3 * wb
    c_uc = c_qb + 4 * wb
    c_gc = c_uc + wc

    xp = x_prompt.reshape(m, d)
    xp_bf = xp.astype(BF16)
    xs = x_sample.reshape(bd, d)
    xs_bf = xs.astype(BF16)

    pka, pva, pkb, pvb, ppool = [], [], [], [], []
    ska, sva, skb, svb, spool = [], [], [], [], []
    for l in range(depth):
        proj = functools.partial(_proj, xp_bf, w_in_bf, l, seq_len=t)
        qa, ka, kmean = proj(c_qa, [(wa, F32, False), (wa, F32, True)], cos_t=cos_p, sin_t=sin_p)
        va, ga = proj(c_va, [(wa, F32, False)] * 2)
        qb_bf, kb, vb, gb = proj(c_qb, [(wb, BF16, False)] + [(wb, F32, False)] * 3)
        uc, gc = proj(c_uc, [(wc, F32, False)] * 2)

        r3 = lambda a: a.reshape(b, t, a.shape[-1])
        mix_a = _moba_prompt(r3(qa), r3(ka), r3(va), kmean.reshape(b, t // MOBA_BLOCK, wa), r3(ga))
        mix_b = _sb_prompt(r3(qb_bf), r3(kb), r3(vb), r3(gb))
        mix_c = _pool_prompt(r3(uc), r3(gc), w_pool, pool_scale, l)
        xp, xp_bf = _merge(mix_a.reshape(m, wa), mix_b.reshape(m, wb), mix_c.reshape(m, wc),
                           w_out_bf, xp, ln_g, ln_b, l, alpha)
        pka.append(ka.reshape(b, t, h_a, D_HEAD))
        pva.append(va.reshape(b, t, h_a, D_HEAD))
        pkb.append(kb.reshape(b, t, h_b, D_HEAD))
        pvb.append(vb.reshape(b, t, h_b, D_HEAD))
        ppool.append(r3(uc)[:, t - POOL_BUF:, :])

        hs = _proj_sample(xs_bf, w_in_bf, l, cos_s, sin_s, 2 * wa)
        qa_s, ka_s = hs[:, c_qa:c_qa + wa], hs[:, c_ka:c_ka + wa]
        va_s, ga_s = hs[:, c_va:c_va + wa], hs[:, c_ga:c_ga + wa]
        qb_s, kb_s = hs[:, c_qb:c_qb + wb], hs[:, c_kb:c_kb + wb]
        vb_s, gb_s = hs[:, c_vb:c_vb + wb], hs[:, c_gb:c_gb + wb]
        uc_s, gc_s = hs[:, c_uc:c_uc + wc], hs[:, c_gc:c_gc + wc]
        ids = _gate_sample(page_table, cache_k_a, qa_s, l)
        ob_s = _sb_sample(page_table, cache_k_b, cache_v_b, qb_s, l)
        oa_s = _moba_sample(page_table, ids, cache_k_a, cache_v_a, qa_s, ka_s, va_s, l)
        xs, xs_bf, new_state = _post_sample(
            oa_s, ga_s, ob_s.reshape(bd, wb), gb_s, uc_s, gc_s, state_t, w_pool, pool_scale,
            w_out_bf, xs, ln_g, ln_b, l, alpha, past_len)
        ska.append(ka_s.reshape(bd, 1, h_a, D_HEAD))
        sva.append(va_s.reshape(bd, 1, h_a, D_HEAD))
        skb.append(kb_s.reshape(bd, 1, h_b, D_HEAD))
        svb.append(vb_s.reshape(bd, 1, h_b, D_HEAD))
        spool.append(new_state.transpose(1, 0, 2))

    return (xp.reshape(b, t, d), xs.reshape(bd, 1, d),
            jnp.stack(pka), jnp.stack(pva), jnp.stack(pkb), jnp.stack(pvb), jnp.stack(ppool),
            jnp.stack(ska), jnp.stack(sva), jnp.stack(skb), jnp.stack(svb), jnp.stack(spool))
```

```python
import functools

import jax
import jax.numpy as jnp
from jax import lax
from jax.experimental import pallas as pl
from jax.experimental.pallas import tpu as pltpu

F32 = jnp.float32
BF16 = jnp.bfloat16
NEG_INF = float("-inf")

D_HEAD = 128
MOBA_BLOCK = 256
MOBA_TOPK = 3
POOL_WINDOWS = (2, 4, 8, 16)
POOL_BUF = max(POOL_WINDOWS) - 1
ROPE_THETA = 10000.0
LN_EPS = 1e-5
LOG2E = 1.4426950408889634

_MIB = 1024 * 1024


def _cparams(semantics, vmem_mib):
    return pltpu.CompilerParams(dimension_semantics=semantics,
                                vmem_limit_bytes=vmem_mib * _MIB)


_NT = (((1,), (1,)), ((), ()))


def _nt_dot(a, b, precision=None):
    return lax.dot_general(a, b, _NT, precision=precision, preferred_element_type=F32)


def _dot(a, b):
    return jnp.dot(a, b, preferred_element_type=F32)


def _split2(x):
    hi = x.astype(BF16)
    lo = (x - hi.astype(F32)).astype(BF16)
    return hi, lo


def _log_sigmoid_pair(z):
    ls = jnp.minimum(z, 0.0) - jnp.log(1.0 + jnp.exp(-jnp.abs(z)))
    return ls, ls - z


def _silu(g):
    return g * jax.nn.sigmoid(g)


def _proj_kernel(*refs, rotary, segs):
    it = iter(refs)
    x_ref = next(it)
    w_ref = next(it)
    cos_ref = next(it) if rotary else None
    sin_ref = next(it) if rotary else None
    x = x_ref[...].astype(BF16)
    c0 = 0
    for ncols, dtype, emit_mean in segs:
        acc = _dot(x, w_ref[:, c0:c0 + ncols])
        c0 += ncols
        if rotary:
            cos = cos_ref[...]
            sin = sin_ref[...]
            parts = []
            for j in range(ncols // D_HEAD):
                xh = acc[:, j * D_HEAD:(j + 1) * D_HEAD]
                parts.append(xh * cos + pltpu.roll(xh, D_HEAD // 2, axis=1) * sin)
            acc = jnp.concatenate(parts, axis=1)
        next(it)[...] = acc.astype(dtype)
        if emit_mean:
            tm = acc.shape[0]
            g = tm // MOBA_BLOCK
            next(it)[...] = jnp.sum(acc.reshape(g, MOBA_BLOCK, ncols), axis=1) * (1.0 / MOBA_BLOCK)


def _proj(x_bf, w_bf, layer, col0, segs, *, seq_len, cos_t=None, sin_t=None, tm=512):
    m, k = x_bf.shape
    total = sum(s[0] for s in segs)
    assert m % tm == 0 and col0 % total == 0 and seq_len % tm == 0 and tm % MOBA_BLOCK == 0
    rotary = cos_t is not None
    n_m = m // tm
    pos_tiles = seq_len // tm
    g = tm // MOBA_BLOCK

    in_specs = [pl.BlockSpec((tm, k), lambda i: (i, 0)),
                pl.BlockSpec((None, k, total), lambda i: (layer, 0, col0 // total))]
    args = [x_bf, w_bf]
    if rotary:
        in_specs += [pl.BlockSpec((tm, D_HEAD), lambda i: (i % pos_tiles, 0))] * 2
        args += [cos_t, sin_t]
    out_shape, out_specs = [], []
    for ncols, dtype, emit_mean in segs:
        out_shape.append(jax.ShapeDtypeStruct((m, ncols), dtype))
        out_specs.append(pl.BlockSpec((tm, ncols), lambda i: (i, 0)))
        if emit_mean:
            out_shape.append(jax.ShapeDtypeStruct((n_m, g, ncols), F32))
            out_specs.append(pl.BlockSpec((None, g, ncols), lambda i: (i, 0, 0)))
    kern = functools.partial(_proj_kernel, rotary=rotary, segs=tuple(segs))
    return pl.pallas_call(
        kern,
        grid=(n_m,),
        in_specs=in_specs,
        out_specs=out_specs,
        out_shape=out_shape,
        compiler_params=_cparams(("arbitrary",), 56),
    )(*args)


def _early_exit_sweep(count, group, carry, block_of, alive):
    def body(st):
        j, _, c = st
        c = group([block_of(j)], c)
        return j + 1, alive(c), c

    _, _, carry = lax.while_loop(lambda st: jnp.logical_and(st[0] < count, st[1]),
                                 body, (jnp.int32(0), alive(carry), carry))
    return carry


SB_DEAD_LOG = -110.0


def _lanes(h):
    return slice(h * D_HEAD, (h + 1) * D_HEAD)


def _fill_kv(k_ref, v_ref, kb_scr, vt_scr):
    hps, nblk, blk, _ = kb_scr.shape
    for h in range(hps):
        for c in range(nblk):
            rows = slice(c * blk, (c + 1) * blk)
            kb_scr[h, c] = k_ref[rows, _lanes(h)].astype(BF16)
            vt_scr[h, c] = v_ref[rows, _lanes(h)].T.astype(BF16)


def _moba_kernel(q_ref, k_ref, v_ref, km_ref, g_ref, o_ref, kb_scr, vt_scr, sel_scr, *, scale):
    i = pl.program_id(2)
    blk = MOBA_BLOCK
    hps = kb_scr.shape[0]
    nb = km_ref.shape[0]

    @pl.when(i == 0)
    def _():
        _fill_kv(k_ref, v_ref, kb_scr, vt_scr)

    keyi = lax.broadcasted_iota(jnp.int32, (blk, blk), 0)
    qi = lax.broadcasted_iota(jnp.int32, (blk, blk), 1)
    rowi = lax.broadcasted_iota(jnp.int32, (nb, blk), 0)
    past = rowi < i

    qbs = []
    for h in range(hps):
        q = q_ref[:, _lanes(h)]
        gate = _nt_dot(km_ref[:, _lanes(h)], q, precision=lax.Precision.HIGHEST)
        gate = jnp.where(past, gate, NEG_INF)
        cnt = jnp.zeros(gate.shape, jnp.int32)
        for j in range(nb):
            gj = gate[j:j + 1, :]
            cnt = cnt + jnp.where(gj > gate, 1,
                                  jnp.where(gj == gate, jnp.where(rowi > j, 1, 0), 0))
        sel_scr[h] = jnp.where(past, jnp.where(cnt < MOBA_TOPK, 1.0, 0.0), 0.0)
        qbs.append((q * (scale * LOG2E)).astype(BF16))

    def scores(h, n):
        s = _nt_dot(kb_scr[h, n], qbs[h])
        if n is i:
            return jnp.where(keyi <= qi, s, NEG_INF)
        return jnp.where(sel_scr[h, pl.ds(n, 1), :] > 0.0, s, NEG_INF)

    def group(ns, carries):
        out = []
        for h in range(hps):
            ss = [scores(h, n) for n in ns]
            m_new = None if carries is None else carries[h][0]
            for s in ss:
                mx = jnp.max(s, axis=0, keepdims=True)
                m_new = mx if m_new is None else jnp.maximum(m_new, mx)
            if carries is None:
                l = jnp.zeros_like(m_new)
                acc = jnp.zeros((D_HEAD, blk), F32)
            else:
                alpha = jnp.exp2(carries[h][0] - m_new)
                l = alpha * carries[h][1]
                acc = alpha * carries[h][2]
            for n, s in zip(ns, ss):
                p = jnp.exp2(s - m_new)
                l = l + jnp.sum(p, axis=0, keepdims=True)
                acc = acc + _dot(vt_scr[h, n], p.astype(BF16))
            out.append((m_new, l, acc))
        return tuple(out)

    quads = lax.shift_right_logical(i, 2)
    first = [lambda _, r=r: group([i] + [4 * quads + k for k in range(r)], None) for r in range(4)]
    carries = lax.switch(jnp.bitwise_and(i, 3), first, 0)
    carries = lax.fori_loop(0, quads, lambda j, c: group([4 * j + k for k in range(4)], c), carries)
    for h in range(hps):
        _, l, acc = carries[h]
        o = (acc / l).T
        o_ref[:, _lanes(h)] = (o * _silu(g_ref[:, _lanes(h)])).astype(BF16)


def _moba_prompt(qa, ka, va, kmean, ga, hps=4):
    b, t, wa = qa.shape
    w = hps * D_HEAD
    assert wa % w == 0
    blk = MOBA_BLOCK
    nq = t // blk
    nb = kmean.shape[1]
    kern = functools.partial(_moba_kernel, scale=D_HEAD ** -0.5)
    tile = pl.BlockSpec((None, blk, w), lambda bi, hi, i: (bi, i, hi))
    full = pl.BlockSpec((None, t, w), lambda bi, hi, i: (bi, 0, hi))
    return pl.pallas_call(
        kern,
        grid=(b, wa // w, nq),
        in_specs=[tile, full, full,
                  pl.BlockSpec((None, nb, w), lambda bi, hi, i: (bi, 0, hi)),
                  tile],
        out_specs=tile,
        out_shape=jax.ShapeDtypeStruct((b, t, wa), BF16),
        scratch_shapes=[pltpu.VMEM((hps, nq, blk, D_HEAD), BF16),
                        pltpu.VMEM((hps, nq, D_HEAD, blk), BF16),
                        pltpu.VMEM((hps, nb, blk), F32)],
        compiler_params=_cparams(("arbitrary",) * 3, 58),
    )(qa, ka, va, kmean, ga)


def _sb_kernel(q_ref, k_ref, v_ref, tri_ref, g_ref, o_ref, kb_scr, vt_scr, *, scale):
    i = pl.program_id(2)
    blk = q_ref.shape[0]
    hps = kb_scr.shape[0]

    @pl.when(i == 0)
    def _():
        _fill_kv(k_ref, v_ref, kb_scr, vt_scr)

    tri = tri_ref[...]
    keyi = lax.broadcasted_iota(jnp.int32, (blk, blk), 0)
    qi = lax.broadcasted_iota(jnp.int32, (blk, blk), 1)
    diag = keyi < qi

    def terms(h, n, mask):
        z = _nt_dot(kb_scr[h, n], q_ref[:, _lanes(h)]) * scale
        ls, lk = _log_sigmoid_pair(z)
        if mask is not None:
            lk = jnp.where(mask, lk, 0.0)
        hi, lo = _split2(lk)
        inner = _dot(tri, hi) + _dot(tri, lo)
        return ls, inner, inner[0:1, :] + lk[0:1, :]

    carries = []
    for h in range(hps):
        ls, inner, carry = terms(h, i, diag)
        w = jnp.where(diag, jnp.exp(ls + inner), 0.0)
        carries.append((carry, _dot(vt_scr[h, i], w.astype(BF16))))

    def group(ns, carries):
        out = []
        for h in range(hps):
            carry, acc = carries[h]
            parts = [terms(h, n, None) for n in ns]
            for n, (ls, inner, tot) in zip(ns, parts):
                w = jnp.exp(ls + (carry + inner))
                acc = acc + _dot(vt_scr[h, n], w.astype(BF16))
                carry = carry + tot
            out.append((carry, acc))
        return tuple(out)

    def alive(carries):
        top = carries[0][0]
        for h in range(1, hps):
            top = jnp.maximum(top, carries[h][0])
        return jnp.max(top) > SB_DEAD_LOG

    carries = _early_exit_sweep(i, group, tuple(carries), lambda v: i - 1 - v, alive)
    for h in range(hps):
        o_ref[:, _lanes(h)] = (carries[h][1].T * _silu(g_ref[:, _lanes(h)])).astype(BF16)


def _sb_prompt(qb_bf, kb, vb, gb, blk=256, hps=2):
    b, t, wb = qb_bf.shape
    w = hps * D_HEAD
    assert wb % w == 0
    nq = t // blk
    r = jnp.arange(blk)
    tri = (r[None, :] > r[:, None]).astype(BF16)
    kern = functools.partial(_sb_kernel, scale=D_HEAD ** -0.5)
    tile = pl.BlockSpec((None, blk, w), lambda bi, hi, i: (bi, i, hi))
    full = pl.BlockSpec((None, t, w), lambda bi, hi, i: (bi, 0, hi))
    return pl.pallas_call(
        kern,
        grid=(b, wb // w, nq),
        in_specs=[tile, full, full,
                  pl.BlockSpec((blk, blk), lambda bi, hi, i: (0, 0)),
                  tile],
        out_specs=tile,
        out_shape=jax.ShapeDtypeStruct((b, t, wb), BF16),
        scratch_shapes=[pltpu.VMEM((hps, nq, blk, D_HEAD), BF16),
                        pltpu.VMEM((hps, nq, D_HEAD, blk), BF16)],
        compiler_params=_cparams(("arbitrary",) * 3, 58),
    )(qb_bf, kb, vb, tri, gb)


def _pool_kernel(u_ref, halo_ref, wp_ref, sc_ref, g_ref, o_ref, xp_ref, *, tq):
    i = pl.program_id(1)
    cg = D_HEAD
    halo = halo_ref[...]
    xp_ref[0:POOL_BUF + 1, :] = jnp.where(i > 0, halo, 0.0)
    xp_ref[POOL_BUF + 1:, :] = u_ref[...]
    pos = i * tq + lax.broadcasted_iota(jnp.int32, (tq, 1), 0)
    outs = []
    for gi, w in enumerate(POOL_WINDOWS):
        sl = slice(gi * cg, (gi + 1) * cg)
        x = xp_ref[POOL_BUF + 1:, sl]
        win = x
        for d in range(1, w):
            win = win + xp_ref[pl.ds(POOL_BUF + 1 - d, tq), sl]
        cnt = jnp.minimum(pos + 1, w).astype(F32)
        r = win / cnt - x
        outs.append(_dot(r.astype(BF16), wp_ref[gi].astype(BF16)))
    mixed = jnp.concatenate(outs, axis=1) * sc_ref[...]
    o_ref[...] = (mixed * _silu(g_ref[...])).astype(BF16)


def _pool_prompt(uc, gc, w_pool, pool_scale, layer, tq=512):
    b, t, wc = uc.shape
    hb = POOL_BUF + 1
    assert t % tq == 0 and tq % hb == 0
    halo_per_tile = tq // hb
    kern = functools.partial(_pool_kernel, tq=tq)
    tile = pl.BlockSpec((None, tq, wc), lambda bi, i: (bi, i, 0))
    return pl.pallas_call(
        kern,
        grid=(b, t // tq),
        in_specs=[tile,
                  pl.BlockSpec((None, hb, wc),
                               lambda bi, i: (bi, jnp.maximum(i * halo_per_tile - 1, 0), 0)),
                  pl.BlockSpec((None,) + w_pool.shape[1:], lambda bi, i: (layer, 0, 0, 0)),
                  pl.BlockSpec((None, 1, wc), lambda bi, i: (layer, 0, 0)),
                  tile],
        out_specs=tile,
        out_shape=jax.ShapeDtypeStruct((b, t, wc), BF16),
        scratch_shapes=[pltpu.VMEM((tq + hb, wc), F32)],
        compiler_params=_cparams(("arbitrary", "arbitrary"), 32),
    )(uc, uc, w_pool, pool_scale.reshape(pool_scale.shape[0], 1, wc), gc)


def _layer_norm(y, g, b):
    mu = jnp.mean(y, axis=-1, keepdims=True)
    d = y - mu
    var = jnp.mean(d * d, axis=-1, keepdims=True)
    return d * lax.rsqrt(var + LN_EPS) * g + b


def _merge_kernel(ma_ref, mb_ref, mc_ref, wa_ref, wb_ref, wc_ref, x_ref, g_ref, b_ref,
                  of_ref, ob_ref, *, alpha):
    y = (alpha * x_ref[...]
         + _dot(ma_ref[...], wa_ref[...])
         + _dot(mb_ref[...], wb_ref[...])
         + _dot(mc_ref[...], wc_ref[...]))
    out = _layer_norm(y, g_ref[...], b_ref[...])
    of_ref[...] = out
    ob_ref[...] = out.astype(BF16)


def _merge(mix_a, mix_b, mix_c, w_out_bf, x, ln_g, ln_b, layer, alpha, tm=512):
    m, d = x.shape
    wa, wb, wc = mix_a.shape[1], mix_b.shape[1], mix_c.shape[1]
    assert wb == wc and wa % wb == 0
    kern = functools.partial(_merge_kernel, alpha=alpha)
    vec = pl.BlockSpec((None, 1, d), lambda i: (layer, 0, 0))
    row = lambda w: pl.BlockSpec((tm, w), lambda i: (i, 0))
    return pl.pallas_call(
        kern,
        grid=(m // tm,),
        in_specs=[row(wa), row(wb), row(wc),
                  pl.BlockSpec((None, wa, d), lambda i: (layer, 0, 0)),
                  pl.BlockSpec((None, wb, d), lambda i: (layer, wa // wb, 0)),
                  pl.BlockSpec((None, wc, d), lambda i: (layer, wa // wb + 1, 0)),
                  row(d), vec, vec],
        out_specs=[row(d), row(d)],
        out_shape=[jax.ShapeDtypeStruct((m, d), F32), jax.ShapeDtypeStruct((m, d), BF16)],
        compiler_params=_cparams(("arbitrary",), 56),
    )(mix_a, mix_b, mix_c, w_out_bf, w_out_bf, w_out_bf, x,
      ln_g.reshape(ln_g.shape[0], 1, d), ln_b.reshape(ln_b.shape[0], 1, d))


def _proj_s_kernel(x_ref, w_ref, cos_ref, sin_ref, o_ref, *, rot_tiles):
    n = pl.program_id(0)
    acc = _dot(x_ref[...], w_ref[...])
    cos = cos_ref[...]
    sin = sin_ref[...]
    parts = []
    for j in range(acc.shape[1] // D_HEAD):
        xh = acc[:, j * D_HEAD:(j + 1) * D_HEAD]
        parts.append(xh * cos + pltpu.roll(xh, D_HEAD // 2, axis=1) * sin)
    rot = jnp.concatenate(parts, axis=1)
    o_ref[...] = jnp.where(n < rot_tiles, rot, acc)


def _proj_sample(xs_bf, w_bf, layer, cos_row, sin_row, rot_cols, tn=512):
    r, k = xs_bf.shape
    ncols = w_bf.shape[2]
    kern = functools.partial(_proj_s_kernel, rot_tiles=rot_cols // tn)
    return pl.pallas_call(
        kern,
        grid=(ncols // tn,),
        in_specs=[pl.BlockSpec((r, k), lambda n: (0, 0)),
                  pl.BlockSpec((None, k, tn), lambda n: (layer, 0, n)),
                  pl.BlockSpec((r, D_HEAD), lambda n: (0, 0)),
                  pl.BlockSpec((r, D_HEAD), lambda n: (0, 0))],
        out_specs=pl.BlockSpec((r, tn), lambda n: (0, n)),
        out_shape=jax.ShapeDtypeStruct((r, ncols), F32),
        compiler_params=_cparams(("arbitrary",), 32),
    )(xs_bf, w_bf, cos_row, sin_row)


def _page_rows(cache):
    depth, n_pool, page, heads, dh = cache.shape
    return cache.reshape(depth, n_pool, page * heads, dh)


def _head_rows(ref, h, page, heads):
    return ref[pl.ds(h, page, stride=heads), :]


def _gate_kernel(pt_ref, *refs, pps, n_steps, page, h_a):
    del pt_ref
    ka_refs = refs[0:pps]
    qa_ref, ids_ref, ksum_ref = refs[pps:]
    s = pl.program_id(1)
    nblk = ksum_ref.shape[0]

    for j in range(0, pps, 2):
        ksum_ref[s * (pps // 2) + j // 2] = (
            jnp.sum(ka_refs[j][...].reshape(page, h_a, D_HEAD), axis=0)
            + jnp.sum(ka_refs[j + 1][...].reshape(page, h_a, D_HEAD), axis=0))

    @pl.when(s == n_steps - 1)
    def _():
        q = qa_ref[...]
        lane = lax.broadcasted_iota(jnp.int32, (h_a, D_HEAD), 1)
        lanef = lane.astype(F32)
        gate = jnp.full((h_a, D_HEAD), NEG_INF, F32)
        for n in range(nblk):
            gn = jnp.sum(ksum_ref[n] * q, axis=1, keepdims=True) * (1.0 / MOBA_BLOCK)
            gate = jnp.where(lane == n, gn, gate)
        ids = jnp.zeros((h_a, D_HEAD), jnp.int32)
        for k in range(MOBA_TOPK):
            mx = jnp.max(gate, axis=1, keepdims=True)
            idx = jnp.min(jnp.where(gate == mx, lanef, float(D_HEAD)), axis=1, keepdims=True)
            ids = jnp.where(lane == k, idx.astype(jnp.int32), ids)
            gate = jnp.where(lanef == idx, NEG_INF, gate)
        ids_ref[...] = ids


def _gate_sample(page_table, cache_k_a, qa_s, layer, pps=16):
    bd, n_pages = page_table.shape
    page, h_a = cache_k_a.shape[2], cache_k_a.shape[3]
    pps = min(pps, n_pages)
    assert n_pages % pps == 0 and pps % 2 == 0 and (page * 2) == MOBA_BLOCK
    n_steps = n_pages // pps
    nblk = n_pages // 2
    assert MOBA_TOPK <= nblk <= D_HEAD

    def page_spec(j):
        return pl.BlockSpec((None, None, page * h_a, D_HEAD),
                            lambda b, s, pt: (layer, pt[b * n_pages + s * pps + j], 0, 0))

    kern = functools.partial(_gate_kernel, pps=pps, n_steps=n_steps, page=page, h_a=h_a)
    return pl.pallas_call(
        kern,
        grid_spec=pltpu.PrefetchScalarGridSpec(
            num_scalar_prefetch=1,
            grid=(bd, n_steps),
            in_specs=([page_spec(j) for j in range(pps)]
                      + [pl.BlockSpec((None, h_a, D_HEAD), lambda b, s, pt: (b, 0, 0))]),
            out_specs=pl.BlockSpec((None, h_a, D_HEAD), lambda b, s, pt: (b, 0, 0)),
            scratch_shapes=[pltpu.VMEM((nblk, h_a, D_HEAD), F32)]),
        out_shape=jax.ShapeDtypeStruct((bd, h_a, D_HEAD), jnp.int32),
        compiler_params=_cparams(("arbitrary", "arbitrary"), 40),
    )(page_table.reshape(-1), *([_page_rows(cache_k_a)] * pps), qa_s.reshape(bd, h_a, D_HEAD))


def _sb_s_kernel(pt_ref, kb_hbm, vb_hbm, qb_ref, tri_ref, ob_ref, kbuf, vbuf, sem, carry_ref, acc_ref,
                 *, layer, n_pages, cpp, page, h_b, scale):
    b = pl.program_id(0)
    n_chunks = n_pages // cpp
    wb = h_b * D_HEAD

    def copies(chunk, slot):
        out = []
        for j in range(cpp):
            pg = pt_ref[b * n_pages + (n_pages - 1 - (chunk * cpp + j))]
            out.append(pltpu.make_async_copy(kb_hbm.at[layer, pg], kbuf.at[slot, j], sem.at[slot, 0, j]))
            out.append(pltpu.make_async_copy(vb_hbm.at[layer, pg], vbuf.at[slot, j], sem.at[slot, 1, j]))
        return out

    def start(chunk, slot):
        for cp in copies(chunk, slot):
            cp.start()

    def wait(chunk, slot):
        for cp in copies(chunk, slot):
            cp.wait()

    carry_ref[...] = jnp.zeros_like(carry_ref)
    acc_ref[...] = jnp.zeros_like(acc_ref)
    r = lax.broadcasted_iota(jnp.int32, (D_HEAD, wb), 0)
    c = lax.broadcasted_iota(jnp.int32, (D_HEAD, wb), 1)
    rtb = jnp.where(r == jnp.right_shift(c, 7),
                    jnp.broadcast_to(qb_ref[...], (D_HEAD, wb)), 0.0).astype(BF16)
    tri = tri_ref[...]

    def alive():
        return jnp.max(carry_ref[...]) > SB_DEAD_LOG

    start(0, 0)

    def body(st):
        chunk, _ = st
        slot = jnp.bitwise_and(chunk, 1)

        @pl.when(chunk + 1 < n_chunks)
        def _():
            start(chunk + 1, 1 - slot)

        wait(chunk, slot)
        for j in range(cpp):
            @pl.when(alive())
            def _(j=j):
                kref = kbuf.at[slot, j]
                vref = vbuf.at[slot, j]
                kb = jnp.concatenate([_head_rows(kref, h, page, h_b) for h in range(h_b)],
                                     axis=1).astype(BF16)
                z = _nt_dot(kb, rtb) * scale
                ls, lk = _log_sigmoid_pair(z)
                hi, lo = _split2(lk)
                inner = _dot(tri, hi) + _dot(tri, lo)
                carry = carry_ref[...]
                w = jnp.exp(ls + (carry + inner))
                carry_ref[...] = carry + (inner[0:1, :] + lk[0:1, :])
                for h in range(h_b):
                    wv = w[:, h:h + 1] * _head_rows(vref, h, page, h_b)
                    acc_ref[h] += jnp.sum(wv.reshape(page // 8, 8, D_HEAD), axis=0)
        return chunk + 1, alive()

    done, _ = lax.while_loop(lambda st: jnp.logical_and(st[0] < n_chunks, st[1]),
                             body, (jnp.int32(0), jnp.bool_(True)))

    @pl.when(done < n_chunks)
    def _():
        wait(done, jnp.bitwise_and(done, 1))

    for h in range(h_b):
        ob_ref[h:h + 1, :] = jnp.sum(acc_ref[h], axis=0, keepdims=True)


def _sb_sample(page_table, cache_k_b, cache_v_b, qb_s, layer, cpp=4):
    bd, n_pages = page_table.shape
    page, h_b = cache_k_b.shape[2], cache_k_b.shape[3]
    wb = h_b * D_HEAD
    cpp = min(cpp, n_pages)
    assert n_pages % cpp == 0 and n_pages >= cpp >= 1
    rows = page * h_b
    r = jnp.arange(page)
    tri = (r[None, :] > r[:, None]).astype(BF16)
    kern = functools.partial(_sb_s_kernel, layer=layer, n_pages=n_pages, cpp=cpp, page=page,
                             h_b=h_b, scale=D_HEAD ** -0.5)
    return pl.pallas_call(
        kern,
        grid_spec=pltpu.PrefetchScalarGridSpec(
            num_scalar_prefetch=1,
            grid=(bd,),
            in_specs=[pl.BlockSpec(memory_space=pl.ANY),
                      pl.BlockSpec(memory_space=pl.ANY),
                      pl.BlockSpec((None, 1, wb), lambda b, pt: (b, 0, 0)),
                      pl.BlockSpec((page, page), lambda b, pt: (0, 0))],
            out_specs=pl.BlockSpec((None, h_b, D_HEAD), lambda b, pt: (b, 0, 0)),
            scratch_shapes=[pltpu.VMEM((2, cpp, rows, D_HEAD), F32),
                            pltpu.VMEM((2, cpp, rows, D_HEAD), F32),
                            pltpu.SemaphoreType.DMA((2, 2, cpp)),
                            pltpu.VMEM((1, D_HEAD), F32),
                            pltpu.VMEM((h_b, 8, D_HEAD), F32)]),
        out_shape=jax.ShapeDtypeStruct((bd, h_b, D_HEAD), F32),
        compiler_params=_cparams(("arbitrary",), 32),
    )(page_table.reshape(-1), _page_rows(cache_k_b), _page_rows(cache_v_b),
      qb_s.reshape(bd, 1, wb), tri)


def _moba_s_kernel(pt_ref, ids_ref, ka_hbm, va_hbm, q_ref, kn_ref, vn_ref, o_ref, kbuf, vbuf, sem,
                   *, layer, n_pages, n_pg, ppb, page, h_a, n_steps, scale):
    step = pl.program_id(0) * h_a + pl.program_id(1)
    slot = jnp.bitwise_and(step, 1)

    def copies(st, sl):
        hh = lax.rem(st, h_a)
        out = []
        for j in range(n_pg):
            kk, half = divmod(j, ppb)
            pg = pt_ref[lax.div(st, h_a) * n_pages + ids_ref[st * MOBA_TOPK + kk] * ppb + half]
            out.append(pltpu.make_async_copy(ka_hbm.at[layer, pg, :, hh, :], kbuf.at[sl, j],
                                             sem.at[sl, 0, j]))
            out.append(pltpu.make_async_copy(va_hbm.at[layer, pg, :, hh, :], vbuf.at[sl, j],
                                             sem.at[sl, 1, j]))
        return out

    @pl.when(step == 0)
    def _():
        for cp in copies(step, slot):
            cp.start()

    @pl.when(step + 1 < n_steps)
    def _():
        for cp in copies(step + 1, 1 - slot):
            cp.start()

    for cp in copies(step, slot):
        cp.wait()

    q = q_ref[...]
    q8 = jnp.broadcast_to(q, (8, D_HEAD)).astype(BF16)
    s_self = jnp.sum(q * kn_ref[...], axis=1, keepdims=True) * scale
    scores = [(_nt_dot(q8, kbuf[slot, j].astype(BF16)) * scale)[0:1, :] for j in range(n_pg)]
    m = s_self
    for sc in scores:
        m = jnp.maximum(m, jnp.max(sc, axis=1, keepdims=True))
    p_self = jnp.exp(s_self - m)
    l = p_self
    acc = p_self * vn_ref[...]
    for j, sc in enumerate(scores):
        p = jnp.exp(sc - m)
        l = l + jnp.sum(p, axis=1, keepdims=True)
        p8 = jnp.broadcast_to(p, (8, p.shape[1])).astype(BF16)
        acc = acc + _dot(p8, vbuf[slot, j].astype(BF16))[0:1, :]
    o_ref[...] = acc / l


def _moba_sample(page_table, ids, cache_k_a, cache_v_a, qa_s, ka_s, va_s, layer):
    bd, n_pages = page_table.shape
    page, h_a = cache_k_a.shape[2], cache_k_a.shape[3]
    wa = h_a * D_HEAD
    ppb = MOBA_BLOCK // page
    n_pg = MOBA_TOPK * ppb
    ids_flat = ids[:, :, :MOBA_TOPK].reshape(-1)
    vec = pl.BlockSpec((None, None, 1, D_HEAD), lambda b, h, pt, idf: (b, h, 0, 0))
    hbm = pl.BlockSpec(memory_space=pl.ANY)
    kern = functools.partial(_moba_s_kernel, layer=layer, n_pages=n_pages, n_pg=n_pg, ppb=ppb,
                             page=page, h_a=h_a, n_steps=bd * h_a, scale=D_HEAD ** -0.5)
    out = pl.pallas_call(
        kern,
        grid_spec=pltpu.PrefetchScalarGridSpec(
            num_scalar_prefetch=2,
            grid=(bd, h_a),
            in_specs=[hbm, hbm, vec, vec, vec],
            out_specs=vec,
            scratch_shapes=[pltpu.VMEM((2, n_pg, page, D_HEAD), F32),
                            pltpu.VMEM((2, n_pg, page, D_HEAD), F32),
                            pltpu.SemaphoreType.DMA((2, 2, n_pg))]),
        out_shape=jax.ShapeDtypeStruct((bd, h_a, 1, D_HEAD), F32),
        compiler_params=_cparams(("arbitrary", "arbitrary"), 32),
    )(page_table.reshape(-1), ids_flat, cache_k_a, cache_v_a,
      qa_s.reshape(bd, h_a, 1, D_HEAD), ka_s.reshape(bd, h_a, 1, D_HEAD),
      va_s.reshape(bd, h_a, 1, D_HEAD))
    return out.reshape(bd, wa)


def _post_s_kernel(oa_ref, ga_ref, ob_ref, gb_ref, u_ref, gc_ref, st_ref, wp_ref, sc_ref,
                   wa_ref, wb_ref, wc_ref, x_ref, g_ref, b_ref,
                   of_ref, obf_ref, ns_ref, *, alpha, pos):
    cg = D_HEAD
    u = u_ref[...]
    outs = []
    for gi, w in enumerate(POOL_WINDOWS):
        sl = slice(gi * cg, (gi + 1) * cg)
        win = u[:, sl]
        for d in range(1, w):
            win = win + st_ref[POOL_BUF - d][:, sl]
        r = win / float(min(pos + 1, w)) - u[:, sl]
        outs.append(_dot(r.astype(BF16), wp_ref[gi].astype(BF16)))
    oc = jnp.concatenate(outs, axis=1) * sc_ref[...]
    ma = (oa_ref[...] * _silu(ga_ref[...])).astype(BF16)
    mb = (ob_ref[...] * _silu(gb_ref[...])).astype(BF16)
    mc = (oc * _silu(gc_ref[...])).astype(BF16)
    y = (alpha * x_ref[...]
         + _dot(ma, wa_ref[...])
         + _dot(mb, wb_ref[...])
         + _dot(mc, wc_ref[...]))
    out = _layer_norm(y, g_ref[...], b_ref[...])
    of_ref[...] = out
    obf_ref[...] = out.astype(BF16)
    for d in range(POOL_BUF - 1):
        ns_ref[d] = st_ref[d + 1]
    ns_ref[POOL_BUF - 1] = u


def _post_sample(oa, ga, ob, gb, uc, gc, state_t, w_pool, pool_scale, w_out_bf, xs,
                 ln_g, ln_b, layer, alpha, pos):
    r, d = xs.shape
    wa, wb, wc = oa.shape[1], ob.shape[1], uc.shape[1]
    kern = functools.partial(_post_s_kernel, alpha=alpha, pos=pos)
    full = lambda a: pl.BlockSpec(a.shape, lambda i: (0,) * a.ndim)
    vec = pl.BlockSpec((None, 1, d), lambda i: (layer, 0, 0))
    return pl.pallas_call(
        kern,
        grid=(1,),
        in_specs=[full(oa), full(ga), full(ob), full(gb), full(uc), full(gc),
                  pl.BlockSpec((None,) + state_t.shape[1:], lambda i: (layer, 0, 0, 0)),
                  pl.BlockSpec((None,) + w_pool.shape[1:], lambda i: (layer, 0, 0, 0)),
                  pl.BlockSpec((None, 1, wc), lambda i: (layer, 0, 0)),
                  pl.BlockSpec((None, wa, d), lambda i: (layer, 0, 0)),
                  pl.BlockSpec((None, wb, d), lambda i: (layer, wa // wb, 0)),
                  pl.BlockSpec((None, wc, d), lambda i: (layer, wa // wb + 1, 0)),
                  full(xs), vec, vec],
        out_specs=[pl.BlockSpec((r, d), lambda i: (0, 0)), pl.BlockSpec((r, d), lambda i: (0, 0)),
                   pl.BlockSpec((POOL_BUF, r, wc), lambda i: (0, 0, 0))],
        out_shape=[jax.ShapeDtypeStruct((r, d), F32), jax.ShapeDtypeStruct((r, d), BF16),
                   jax.ShapeDtypeStruct((POOL_BUF, r, wc), F32)],
        compiler_params=_cparams(("arbitrary",), 48),
    )(oa, ga, ob, gb, uc, gc, state_t, w_pool,
      pool_scale.reshape(pool_scale.shape[0], 1, wc), w_out_bf, w_out_bf, w_out_bf, xs,
      ln_g.reshape(ln_g.shape[0], 1, d), ln_b.reshape(ln_b.shape[0], 1, d))


def _rope_tables(pos):
    half = D_HEAD // 2
    inv = 1.0 / (ROPE_THETA ** (jnp.arange(half, dtype=F32) * (2.0 / D_HEAD)))
    ang = pos.astype(F32)[:, None] * inv[None, :]
    cos, sin = jnp.cos(ang), jnp.sin(ang)
    return jnp.concatenate([cos, cos], axis=1), jnp.concatenate([-sin, sin], axis=1)


def kernel(x_prompt, x_sample, cache_k_a, cache_v_a, cache_k_b, cache_v_b, state_pool,
           page_table, w_in, w_pool, pool_scale, w_out, ln_g, ln_b):
    b, t, d = x_prompt.shape
    bd, ts, _ = x_sample.shape
    assert ts == 1
    depth = w_in.shape[0]
    h_a, h_b = cache_k_a.shape[3], cache_k_b.shape[3]
    wa, wb = h_a * D_HEAD, h_b * D_HEAD
    wc = state_pool.shape[3]
    assert w_in.shape[2] == 4 * wa + 4 * wb + 2 * wc
    n_pages = page_table.shape[1]
    past_len = n_pages * cache_k_a.shape[2]
    alpha = (2 * depth) ** 0.25
    m = b * t

    w_in_bf = w_in.astype(BF16)
    w_out_bf = w_out.astype(BF16)
    cos_p, sin_p = _rope_tables(jnp.arange(t, dtype=jnp.int32))
    cos_s, sin_s = _rope_tables(jnp.full((bd,), past_len, jnp.int32))
    state_t = state_pool.transpose(0, 2, 1, 3)

    c_qa, c_ka, c_va, c_ga = 0, wa, 2 * wa, 3 * wa
    c_qb = 4 * wa
    c_kb, c_vb, c_gb = c_qb + wb, c_qb + 2 * wb, c_qb + 3 * wb
    c_uc = c_qb + 4 * wb
    c_gc = c_uc + wc

    xp = x_prompt.reshape(m, d)
    xp_bf = xp
    xs = x_sample.reshape(bd, d)
    xs_bf = xs.astype(BF16)

    pka, pva, pkb, pvb, ppool = [], [], [], [], []
    ska, sva, skb, svb, spool = [], [], [], [], []
    for l in range(depth):
        proj = functools.partial(_proj, xp_bf, w_in_bf, l, seq_len=t)
        qa, ka, kmean = proj(c_qa, [(wa, F32, False), (wa, F32, True)], cos_t=cos_p, sin_t=sin_p)
        va, ga = proj(c_va, [(wa, F32, False)] * 2)
        qb_bf, kb, vb, gb = proj(c_qb, [(wb, BF16, False)] + [(wb, F32, False)] * 3)
        uc, gc = proj(c_uc, [(wc, F32, False)] * 2)

        r3 = lambda a: a.reshape(b, t, a.shape[-1])
        mix_a = _moba_prompt(r3(qa), r3(ka), r3(va), kmean.reshape(b, t // MOBA_BLOCK, wa), r3(ga))
        mix_b = _sb_prompt(r3(qb_bf), r3(kb), r3(vb), r3(gb))
        mix_c = _pool_prompt(r3(uc), r3(gc), w_pool, pool_scale, l)
        xp, xp_bf = _merge(mix_a.reshape(m, wa), mix_b.reshape(m, wb), mix_c.reshape(m, wc),
                           w_out_bf, xp, ln_g, ln_b, l, alpha)
        pka.append(ka.reshape(b, t, h_a, D_HEAD))
        pva.append(va.reshape(b, t, h_a, D_HEAD))
        pkb.append(kb.reshape(b, t, h_b, D_HEAD))
        pvb.append(vb.reshape(b, t, h_b, D_HEAD))
        ppool.append(r3(uc)[:, t - POOL_BUF:, :])

        hs = _proj_sample(xs_bf, w_in_bf, l, cos_s, sin_s, 2 * wa)
        qa_s, ka_s = hs[:, c_qa:c_qa + wa], hs[:, c_ka:c_ka + wa]
        va_s, ga_s = hs[:, c_va:c_va + wa], hs[:, c_ga:c_ga + wa]
        qb_s, kb_s = hs[:, c_qb:c_qb + wb], hs[:, c_kb:c_kb + wb]
        vb_s, gb_s = hs[:, c_vb:c_vb + wb], hs[:, c_gb:c_gb + wb]
        uc_s, gc_s = hs[:, c_uc:c_uc + wc], hs[:, c_gc:c_gc + wc]
        ids = _gate_sample(page_table, cache_k_a, qa_s, l)
        ob_s = _sb_sample(page_table, cache_k_b, cache_v_b, qb_s, l)
        oa_s = _moba_sample(page_table, ids, cache_k_a, cache_v_a, qa_s, ka_s, va_s, l)
        xs, xs_bf, new_state = _post_sample(
            oa_s, ga_s, ob_s.reshape(bd, wb), gb_s, uc_s, gc_s, state_t, w_pool, pool_scale,
            w_out_bf, xs, ln_g, ln_b, l, alpha, past_len)
        ska.append(ka_s.reshape(bd, 1, h_a, D_HEAD))
        sva.append(va_s.reshape(bd, 1, h_a, D_HEAD))
        skb.append(kb_s.reshape(bd, 1, h_b, D_HEAD))
        svb.append(vb_s.reshape(bd, 1, h_b, D_HEAD))
        spool.append(new_state.transpose(1, 0, 2))

    return (xp.reshape(b, t, d), xs.reshape(bd, 1, d),
            jnp.stack(pka), jnp.stack(pva), jnp.stack(pkb), jnp.stack(pvb), jnp.stack(ppool),
            jnp.stack(ska), jnp.stack(sva), jnp.stack(skb), jnp.stack(svb), jnp.stack(spool))
```

```python
import functools

import jax
import jax.numpy as jnp
from jax import lax
from jax.experimental import pallas as pl
from jax.experimental.pallas import tpu as pltpu

F32 = jnp.float32
BF16 = jnp.bfloat16
NEG_INF = float("-inf")

D_HEAD = 128
MOBA_BLOCK = 256
MOBA_TOPK = 3
POOL_WINDOWS = (2, 4, 8, 16)
POOL_BUF = max(POOL_WINDOWS) - 1
ROPE_THETA = 10000.0
LN_EPS = 1e-5
LOG2E = 1.4426950408889634

_MIB = 1024 * 1024


def _cparams(semantics, vmem_mib):
    return pltpu.CompilerParams(dimension_semantics=semantics,
                                vmem_limit_bytes=vmem_mib * _MIB)


_NT = (((1,), (1,)), ((), ()))


def _nt_dot(a, b, precision=None):
    return lax.dot_general(a, b, _NT, precision=precision, preferred_element_type=F32)


def _dot(a, b):
    return jnp.dot(a, b, preferred_element_type=F32)


def _split2(x):
    hi = x.astype(BF16)
    lo = (x - hi.astype(F32)).astype(BF16)
    return hi, lo


def _log_sigmoid_pair(z):
    ls = jnp.minimum(z, 0.0) - jnp.log(1.0 + jnp.exp(-jnp.abs(z)))
    return ls, ls - z


def _silu(g):
    return g * jax.nn.sigmoid(g)


def _proj_kernel(*refs, rotary, segs):
    it = iter(refs)
    x_ref = next(it)
    w_ref = next(it)
    cos_ref = next(it) if rotary else None
    sin_ref = next(it) if rotary else None
    x = x_ref[...].astype(BF16)
    c0 = 0
    for ncols, dtype, emit_mean in segs:
        acc = _dot(x, w_ref[:, c0:c0 + ncols].astype(BF16))
        c0 += ncols
        if rotary:
            cos = cos_ref[...]
            sin = sin_ref[...]
            parts = []
            for j in range(ncols // D_HEAD):
                xh = acc[:, j * D_HEAD:(j + 1) * D_HEAD]
                parts.append(xh * cos + pltpu.roll(xh, D_HEAD // 2, axis=1) * sin)
            acc = jnp.concatenate(parts, axis=1)
        next(it)[...] = acc.astype(dtype)
        if emit_mean:
            tm = acc.shape[0]
            g = tm // MOBA_BLOCK
            next(it)[...] = jnp.sum(acc.reshape(g, MOBA_BLOCK, ncols), axis=1) * (1.0 / MOBA_BLOCK)


def _proj(x_bf, w_bf, layer, col0, segs, *, seq_len, cos_t=None, sin_t=None, tm=512):
    m, k = x_bf.shape
    total = sum(s[0] for s in segs)
    assert m % tm == 0 and col0 % total == 0 and seq_len % tm == 0 and tm % MOBA_BLOCK == 0
    rotary = cos_t is not None
    n_m = m // tm
    pos_tiles = seq_len // tm
    g = tm // MOBA_BLOCK

    in_specs = [pl.BlockSpec((tm, k), lambda i: (i, 0)),
                pl.BlockSpec((None, k, total), lambda i: (layer, 0, col0 // total),
                             pipeline_mode=pl.Buffered(1))]
    args = [x_bf, w_bf]
    if rotary:
        in_specs += [pl.BlockSpec((tm, D_HEAD), lambda i: (i % pos_tiles, 0))] * 2
        args += [cos_t, sin_t]
    out_shape, out_specs = [], []
    for ncols, dtype, emit_mean in segs:
        out_shape.append(jax.ShapeDtypeStruct((m, ncols), dtype))
        out_specs.append(pl.BlockSpec((tm, ncols), lambda i: (i, 0)))
        if emit_mean:
            out_shape.append(jax.ShapeDtypeStruct((n_m, g, ncols), F32))
            out_specs.append(pl.BlockSpec((None, g, ncols), lambda i: (i, 0, 0)))
    kern = functools.partial(_proj_kernel, rotary=rotary, segs=tuple(segs))
    return pl.pallas_call(
        kern,
        grid=(n_m,),
        in_specs=in_specs,
        out_specs=out_specs,
        out_shape=out_shape,
        compiler_params=_cparams(("arbitrary",), 56),
    )(*args)


def _early_exit_sweep(count, group, carry, block_of, alive):
    def body(st):
        j, _, c = st
        c = group([block_of(j)], c)
        return j + 1, alive(c), c

    _, _, carry = lax.while_loop(lambda st: jnp.logical_and(st[0] < count, st[1]),
                                 body, (jnp.int32(0), alive(carry), carry))
    return carry


SB_DEAD_LOG = -110.0


def _lanes(h):
    return slice(h * D_HEAD, (h + 1) * D_HEAD)


def _fill_kv(k_ref, v_ref, kb_scr, vt_scr):
    hps, nblk, blk, _ = kb_scr.shape
    for h in range(hps):
        for c in range(nblk):
            rows = slice(c * blk, (c + 1) * blk)
            kb_scr[h, c] = k_ref[rows, _lanes(h)].astype(BF16)
            vt_scr[h, c] = v_ref[rows, _lanes(h)].T.astype(BF16)


def _moba_kernel(q_ref, k_ref, v_ref, km_ref, g_ref, o_ref, kb_scr, vt_scr, sel_scr, *, scale):
    i = pl.program_id(2)
    blk = MOBA_BLOCK
    hps = kb_scr.shape[0]
    nb = km_ref.shape[0]

    @pl.when(i == 0)
    def _():
        _fill_kv(k_ref, v_ref, kb_scr, vt_scr)

    keyi = lax.broadcasted_iota(jnp.int32, (blk, blk), 0)
    qi = lax.broadcasted_iota(jnp.int32, (blk, blk), 1)
    rowi = lax.broadcasted_iota(jnp.int32, (nb, blk), 0)
    past = rowi < i

    qbs = []
    for h in range(hps):
        q = q_ref[:, _lanes(h)]
        gate = _nt_dot(km_ref[:, _lanes(h)], q, precision=lax.Precision.HIGHEST)
        gate = jnp.where(past, gate, NEG_INF)
        cnt = jnp.zeros(gate.shape, jnp.int32)
        for j in range(nb):
            gj = gate[j:j + 1, :]
            cnt = cnt + jnp.where(gj > gate, 1,
                                  jnp.where(gj == gate, jnp.where(rowi > j, 1, 0), 0))
        sel_scr[h] = jnp.where(past, jnp.where(cnt < MOBA_TOPK, 1.0, 0.0), 0.0)
        qbs.append((q * (scale * LOG2E)).astype(BF16))

    def scores(h, n):
        s = _nt_dot(kb_scr[h, n], qbs[h])
        if n is i:
            return jnp.where(keyi <= qi, s, NEG_INF)
        return jnp.where(sel_scr[h, pl.ds(n, 1), :] > 0.0, s, NEG_INF)

    def group(ns, carries):
        out = []
        for h in range(hps):
            ss = [scores(h, n) for n in ns]
            m_new = None if carries is None else carries[h][0]
            for s in ss:
                mx = jnp.max(s, axis=0, keepdims=True)
                m_new = mx if m_new is None else jnp.maximum(m_new, mx)
            if carries is None:
                l = jnp.zeros_like(m_new)
                acc = jnp.zeros((D_HEAD, blk), F32)
            else:
                alpha = jnp.exp2(carries[h][0] - m_new)
                l = alpha * carries[h][1]
                acc = alpha * carries[h][2]
            for n, s in zip(ns, ss):
                p = jnp.exp2(s - m_new)
                l = l + jnp.sum(p, axis=0, keepdims=True)
                acc = acc + _dot(vt_scr[h, n], p.astype(BF16))
            out.append((m_new, l, acc))
        return tuple(out)

    quads = lax.shift_right_logical(i, 2)
    first = [lambda _, r=r: group([i] + [4 * quads + k for k in range(r)], None) for r in range(4)]
    carries = lax.switch(jnp.bitwise_and(i, 3), first, 0)
    carries = lax.fori_loop(0, quads, lambda j, c: group([4 * j + k for k in range(4)], c), carries)
    for h in range(hps):
        _, l, acc = carries[h]
        o = (acc / l).T
        o_ref[:, _lanes(h)] = (o * _silu(g_ref[:, _lanes(h)])).astype(BF16)


def _moba_prompt(qa, ka, va, kmean, ga, hps=4):
    b, t, wa = qa.shape
    w = hps * D_HEAD
    assert wa % w == 0
    blk = MOBA_BLOCK
    nq = t // blk
    nb = kmean.shape[1]
    kern = functools.partial(_moba_kernel, scale=D_HEAD ** -0.5)
    tile = pl.BlockSpec((None, blk, w), lambda bi, hi, i: (bi, i, hi))
    full = pl.BlockSpec((None, t, w), lambda bi, hi, i: (bi, 0, hi))
    return pl.pallas_call(
        kern,
        grid=(b, wa // w, nq),
        in_specs=[tile, full, full,
                  pl.BlockSpec((None, nb, w), lambda bi, hi, i: (bi, 0, hi)),
                  tile],
        out_specs=tile,
        out_shape=jax.ShapeDtypeStruct((b, t, wa), BF16),
        scratch_shapes=[pltpu.VMEM((hps, nq, blk, D_HEAD), BF16),
                        pltpu.VMEM((hps, nq, D_HEAD, blk), BF16),
                        pltpu.VMEM((hps, nb, blk), F32)],
        compiler_params=_cparams(("arbitrary",) * 3, 58),
    )(qa, ka, va, kmean, ga)


def _sb_kernel(q_ref, k_ref, v_ref, tri_ref, g_ref, o_ref, kb_scr, vt_scr, *, scale):
    i = pl.program_id(2)
    blk = q_ref.shape[0]
    hps = kb_scr.shape[0]

    @pl.when(i == 0)
    def _():
        _fill_kv(k_ref, v_ref, kb_scr, vt_scr)

    tri = tri_ref[...]
    keyi = lax.broadcasted_iota(jnp.int32, (blk, blk), 0)
    qi = lax.broadcasted_iota(jnp.int32, (blk, blk), 1)
    diag = keyi < qi

    def terms(h, n, mask):
        z = _nt_dot(kb_scr[h, n], q_ref[:, _lanes(h)]) * scale
        ls, lk = _log_sigmoid_pair(z)
        if mask is not None:
            lk = jnp.where(mask, lk, 0.0)
        hi, lo = _split2(lk)
        inner = _dot(tri, hi) + _dot(tri, lo)
        return ls, inner, inner[0:1, :] + lk[0:1, :]

    carries = []
    for h in range(hps):
        ls, inner, carry = terms(h, i, diag)
        w = jnp.where(diag, jnp.exp(ls + inner), 0.0)
        carries.append((carry, _dot(vt_scr[h, i], w.astype(BF16))))

    def group(ns, carries):
        out = []
        for h in range(hps):
            carry, acc = carries[h]
            parts = [terms(h, n, None) for n in ns]
            for n, (ls, inner, tot) in zip(ns, parts):
                w = jnp.exp(ls + (carry + inner))
                acc = acc + _dot(vt_scr[h, n], w.astype(BF16))
                carry = carry + tot
            out.append((carry, acc))
        return tuple(out)

    def alive(carries):
        top = carries[0][0]
        for h in range(1, hps):
            top = jnp.maximum(top, carries[h][0])
        return jnp.max(top) > SB_DEAD_LOG

    carries = _early_exit_sweep(i, group, tuple(carries), lambda v: i - 1 - v, alive)
    for h in range(hps):
        o_ref[:, _lanes(h)] = (carries[h][1].T * _silu(g_ref[:, _lanes(h)])).astype(BF16)


def _sb_prompt(qb_bf, kb, vb, gb, blk=256, hps=2):
    b, t, wb = qb_bf.shape
    w = hps * D_HEAD
    assert wb % w == 0
    nq = t // blk
    r = jnp.arange(blk)
    tri = (r[None, :] > r[:, None]).astype(BF16)
    kern = functools.partial(_sb_kernel, scale=D_HEAD ** -0.5)
    tile = pl.BlockSpec((None, blk, w), lambda bi, hi, i: (bi, i, hi))
    full = pl.BlockSpec((None, t, w), lambda bi, hi, i: (bi, 0, hi))
    return pl.pallas_call(
        kern,
        grid=(b, wb // w, nq),
        in_specs=[tile, full, full,
                  pl.BlockSpec((blk, blk), lambda bi, hi, i: (0, 0)),
                  tile],
        out_specs=tile,
        out_shape=jax.ShapeDtypeStruct((b, t, wb), BF16),
        scratch_shapes=[pltpu.VMEM((hps, nq, blk, D_HEAD), BF16),
                        pltpu.VMEM((hps, nq, D_HEAD, blk), BF16)],
        compiler_params=_cparams(("arbitrary",) * 3, 58),
    )(qb_bf, kb, vb, tri, gb)


def _pool_kernel(u_ref, halo_ref, wp_ref, sc_ref, g_ref, o_ref, xp_ref, *, tq):
    i = pl.program_id(1)
    cg = D_HEAD
    halo = halo_ref[...]
    xp_ref[0:POOL_BUF + 1, :] = jnp.where(i > 0, halo, 0.0)
    xp_ref[POOL_BUF + 1:, :] = u_ref[...]
    pos = i * tq + lax.broadcasted_iota(jnp.int32, (tq, 1), 0)
    outs = []
    for gi, w in enumerate(POOL_WINDOWS):
        sl = slice(gi * cg, (gi + 1) * cg)
        x = xp_ref[POOL_BUF + 1:, sl]
        win = x
        for d in range(1, w):
            win = win + xp_ref[pl.ds(POOL_BUF + 1 - d, tq), sl]
        cnt = jnp.minimum(pos + 1, w).astype(F32)
        r = win / cnt - x
        outs.append(_dot(r.astype(BF16), wp_ref[gi].astype(BF16)))
    mixed = jnp.concatenate(outs, axis=1) * sc_ref[...]
    o_ref[...] = (mixed * _silu(g_ref[...])).astype(BF16)


def _pool_prompt(uc, gc, w_pool, pool_scale, layer, tq=512):
    b, t, wc = uc.shape
    hb = POOL_BUF + 1
    assert t % tq == 0 and tq % hb == 0
    halo_per_tile = tq // hb
    kern = functools.partial(_pool_kernel, tq=tq)
    tile = pl.BlockSpec((None, tq, wc), lambda bi, i: (bi, i, 0))
    return pl.pallas_call(
        kern,
        grid=(b, t // tq),
        in_specs=[tile,
                  pl.BlockSpec((None, hb, wc),
                               lambda bi, i: (bi, jnp.maximum(i * halo_per_tile - 1, 0), 0)),
                  pl.BlockSpec((None,) + w_pool.shape[1:], lambda bi, i: (layer, 0, 0, 0)),
                  pl.BlockSpec((None, 1, wc), lambda bi, i: (layer, 0, 0)),
                  tile],
        out_specs=tile,
        out_shape=jax.ShapeDtypeStruct((b, t, wc), BF16),
        scratch_shapes=[pltpu.VMEM((tq + hb, wc), F32)],
        compiler_params=_cparams(("arbitrary", "arbitrary"), 32),
    )(uc, uc, w_pool, pool_scale.reshape(pool_scale.shape[0], 1, wc), gc)


def _layer_norm(y, g, b):
    mu = jnp.mean(y, axis=-1, keepdims=True)
    d = y - mu
    var = jnp.mean(d * d, axis=-1, keepdims=True)
    return d * lax.rsqrt(var + LN_EPS) * g + b


def _merge_kernel(ma_ref, mb_ref, mc_ref, wa_ref, wb_ref, wc_ref, x_ref, g_ref, b_ref,
                  of_ref, ob_ref, *, alpha):
    y = (alpha * x_ref[...]
         + _dot(ma_ref[...], wa_ref[...])
         + _dot(mb_ref[...], wb_ref[...])
         + _dot(mc_ref[...], wc_ref[...]))
    out = _layer_norm(y, g_ref[...], b_ref[...])
    of_ref[...] = out
    ob_ref[...] = out.astype(BF16)


def _merge(mix_a, mix_b, mix_c, w_out_bf, x, ln_g, ln_b, layer, alpha, tm=512):
    m, d = x.shape
    wa, wb, wc = mix_a.shape[1], mix_b.shape[1], mix_c.shape[1]
    assert wb == wc and wa % wb == 0
    kern = functools.partial(_merge_kernel, alpha=alpha)
    vec = pl.BlockSpec((None, 1, d), lambda i: (layer, 0, 0))
    row = lambda w: pl.BlockSpec((tm, w), lambda i: (i, 0))
    return pl.pallas_call(
        kern,
        grid=(m // tm,),
        in_specs=[row(wa), row(wb), row(wc),
                  pl.BlockSpec((None, wa, d), lambda i: (layer, 0, 0)),
                  pl.BlockSpec((None, wb, d), lambda i: (layer, wa // wb, 0)),
                  pl.BlockSpec((None, wc, d), lambda i: (layer, wa // wb + 1, 0)),
                  row(d), vec, vec],
        out_specs=[row(d), row(d)],
        out_shape=[jax.ShapeDtypeStruct((m, d), F32), jax.ShapeDtypeStruct((m, d), BF16)],
        compiler_params=_cparams(("arbitrary",), 56),
    )(mix_a, mix_b, mix_c, w_out_bf, w_out_bf, w_out_bf, x,
      ln_g.reshape(ln_g.shape[0], 1, d), ln_b.reshape(ln_b.shape[0], 1, d))


def _proj_s_kernel(x_ref, w_ref, cos_ref, sin_ref, o_ref, *, rot_tiles):
    n = pl.program_id(0)
    acc = _dot(x_ref[...], w_ref[...].astype(BF16))
    cos = cos_ref[...]
    sin = sin_ref[...]
    parts = []
    for j in range(acc.shape[1] // D_HEAD):
        xh = acc[:, j * D_HEAD:(j + 1) * D_HEAD]
        parts.append(xh * cos + pltpu.roll(xh, D_HEAD // 2, axis=1) * sin)
    rot = jnp.concatenate(parts, axis=1)
    o_ref[...] = jnp.where(n < rot_tiles, rot, acc)


def _proj_sample(xs_bf, w_bf, layer, cos_row, sin_row, rot_cols, tn=512):
    r, k = xs_bf.shape
    ncols = w_bf.shape[2]
    kern = functools.partial(_proj_s_kernel, rot_tiles=rot_cols // tn)
    return pl.pallas_call(
        kern,
        grid=(ncols // tn,),
        in_specs=[pl.BlockSpec((r, k), lambda n: (0, 0)),
                  pl.BlockSpec((None, k, tn), lambda n: (layer, 0, n)),
                  pl.BlockSpec((r, D_HEAD), lambda n: (0, 0)),
                  pl.BlockSpec((r, D_HEAD), lambda n: (0, 0))],
        out_specs=pl.BlockSpec((r, tn), lambda n: (0, n)),
        out_shape=jax.ShapeDtypeStruct((r, ncols), F32),
        compiler_params=_cparams(("arbitrary",), 32),
    )(xs_bf, w_bf, cos_row, sin_row)


def _page_rows(cache):
    depth, n_pool, page, heads, dh = cache.shape
    return cache.reshape(depth, n_pool, page * heads, dh)


def _head_rows(ref, h, page, heads):
    return ref[pl.ds(h, page, stride=heads), :]


def _gate_kernel(pt_ref, *refs, pps, n_steps, page, h_a):
    del pt_ref
    ka_refs = refs[0:pps]
    qa_ref, ids_ref, ksum_ref = refs[pps:]
    s = pl.program_id(1)
    nblk = ksum_ref.shape[0]

    for j in range(0, pps, 2):
        ksum_ref[s * (pps // 2) + j // 2] = (
            jnp.sum(ka_refs[j][...].reshape(page, h_a, D_HEAD), axis=0)
            + jnp.sum(ka_refs[j + 1][...].reshape(page, h_a, D_HEAD), axis=0))

    @pl.when(s == n_steps - 1)
    def _():
        q = qa_ref[...]
        lane = lax.broadcasted_iota(jnp.int32, (h_a, D_HEAD), 1)
        lanef = lane.astype(F32)
        gate = jnp.full((h_a, D_HEAD), NEG_INF, F32)
        for n in range(nblk):
            gn = jnp.sum(ksum_ref[n] * q, axis=1, keepdims=True) * (1.0 / MOBA_BLOCK)
            gate = jnp.where(lane == n, gn, gate)
        ids = jnp.zeros((h_a, D_HEAD), jnp.int32)
        for k in range(MOBA_TOPK):
            mx = jnp.max(gate, axis=1, keepdims=True)
            idx = jnp.min(jnp.where(gate == mx, lanef, float(D_HEAD)), axis=1, keepdims=True)
            ids = jnp.where(lane == k, idx.astype(jnp.int32), ids)
            gate = jnp.where(lanef == idx, NEG_INF, gate)
        ids_ref[...] = ids


def _gate_sample(page_table, cache_k_a, qa_s, layer, pps=16):
    bd, n_pages = page_table.shape
    page, h_a = cache_k_a.shape[2], cache_k_a.shape[3]
    pps = min(pps, n_pages)
    assert n_pages % pps == 0 and pps % 2 == 0 and (page * 2) == MOBA_BLOCK
    n_steps = n_pages // pps
    nblk = n_pages // 2
    assert MOBA_TOPK <= nblk <= D_HEAD

    def page_spec(j):
        return pl.BlockSpec((None, None, page * h_a, D_HEAD),
                            lambda b, s, pt: (layer, pt[b * n_pages + s * pps + j], 0, 0))

    kern = functools.partial(_gate_kernel, pps=pps, n_steps=n_steps, page=page, h_a=h_a)
    return pl.pallas_call(
        kern,
        grid_spec=pltpu.PrefetchScalarGridSpec(
            num_scalar_prefetch=1,
            grid=(bd, n_steps),
            in_specs=([page_spec(j) for j in range(pps)]
                      + [pl.BlockSpec((None, h_a, D_HEAD), lambda b, s, pt: (b, 0, 0))]),
            out_specs=pl.BlockSpec((None, h_a, D_HEAD), lambda b, s, pt: (b, 0, 0)),
            scratch_shapes=[pltpu.VMEM((nblk, h_a, D_HEAD), F32)]),
        out_shape=jax.ShapeDtypeStruct((bd, h_a, D_HEAD), jnp.int32),
        compiler_params=_cparams(("arbitrary", "arbitrary"), 40),
    )(page_table.reshape(-1), *([_page_rows(cache_k_a)] * pps), qa_s.reshape(bd, h_a, D_HEAD))


def _sb_s_kernel(pt_ref, kb_hbm, vb_hbm, qb_ref, tri_ref, ob_ref, kbuf, vbuf, sem, carry_ref, acc_ref,
                 *, layer, n_pages, cpp, page, h_b, scale):
    b = pl.program_id(0)
    n_chunks = n_pages // cpp
    wb = h_b * D_HEAD

    def copies(chunk, slot):
        out = []
        for j in range(cpp):
            pg = pt_ref[b * n_pages + (n_pages - 1 - (chunk * cpp + j))]
            out.append(pltpu.make_async_copy(kb_hbm.at[layer, pg], kbuf.at[slot, j], sem.at[slot, 0, j]))
            out.append(pltpu.make_async_copy(vb_hbm.at[layer, pg], vbuf.at[slot, j], sem.at[slot, 1, j]))
        return out

    def start(chunk, slot):
        for cp in copies(chunk, slot):
            cp.start()

    def wait(chunk, slot):
        for cp in copies(chunk, slot):
            cp.wait()

    carry_ref[...] = jnp.zeros_like(carry_ref)
    acc_ref[...] = jnp.zeros_like(acc_ref)
    r = lax.broadcasted_iota(jnp.int32, (D_HEAD, wb), 0)
    c = lax.broadcasted_iota(jnp.int32, (D_HEAD, wb), 1)
    rtb = jnp.where(r == jnp.right_shift(c, 7),
                    jnp.broadcast_to(qb_ref[...], (D_HEAD, wb)), 0.0).astype(BF16)
    tri = tri_ref[...]

    def alive():
        return jnp.max(carry_ref[...]) > SB_DEAD_LOG

    start(0, 0)

    def body(st):
        chunk, _ = st
        slot = jnp.bitwise_and(chunk, 1)

        @pl.when(chunk + 1 < n_chunks)
        def _():
            start(chunk + 1, 1 - slot)

        wait(chunk, slot)
        for j in range(cpp):
            @pl.when(alive())
            def _(j=j):
                kref = kbuf.at[slot, j]
                vref = vbuf.at[slot, j]
                kb = jnp.concatenate([_head_rows(kref, h, page, h_b) for h in range(h_b)],
                                     axis=1).astype(BF16)
                z = _nt_dot(kb, rtb) * scale
                ls, lk = _log_sigmoid_pair(z)
                hi, lo = _split2(lk)
                inner = _dot(tri, hi) + _dot(tri, lo)
                carry = carry_ref[...]
                w = jnp.exp(ls + (carry + inner))
                carry_ref[...] = carry + (inner[0:1, :] + lk[0:1, :])
                for h in range(h_b):
                    wv = w[:, h:h + 1] * _head_rows(vref, h, page, h_b)
                    acc_ref[h] += jnp.sum(wv.reshape(page // 8, 8, D_HEAD), axis=0)
        return chunk + 1, alive()

    done, _ = lax.while_loop(lambda st: jnp.logical_and(st[0] < n_chunks, st[1]),
                             body, (jnp.int32(0), jnp.bool_(True)))

    @pl.when(done < n_chunks)
    def _():
        wait(done, jnp.bitwise_and(done, 1))

    for h in range(h_b):
        ob_ref[h:h + 1, :] = jnp.sum(acc_ref[h], axis=0, keepdims=True)


def _sb_sample(page_table, cache_k_b, cache_v_b, qb_s, layer, cpp=4):
    bd, n_pages = page_table.shape
    page, h_b = cache_k_b.shape[2], cache_k_b.shape[3]
    wb = h_b * D_HEAD
    cpp = min(cpp, n_pages)
    assert n_pages % cpp == 0 and n_pages >= cpp >= 1
    rows = page * h_b
    r = jnp.arange(page)
    tri = (r[None, :] > r[:, None]).astype(BF16)
    kern = functools.partial(_sb_s_kernel, layer=layer, n_pages=n_pages, cpp=cpp, page=page,
                             h_b=h_b, scale=D_HEAD ** -0.5)
    return pl.pallas_call(
        kern,
        grid_spec=pltpu.PrefetchScalarGridSpec(
            num_scalar_prefetch=1,
            grid=(bd,),
            in_specs=[pl.BlockSpec(memory_space=pl.ANY),
                      pl.BlockSpec(memory_space=pl.ANY),
                      pl.BlockSpec((None, 1, wb), lambda b, pt: (b, 0, 0)),
                      pl.BlockSpec((page, page), lambda b, pt: (0, 0))],
            out_specs=pl.BlockSpec((None, h_b, D_HEAD), lambda b, pt: (b, 0, 0)),
            scratch_shapes=[pltpu.VMEM((2, cpp, rows, D_HEAD), F32),
                            pltpu.VMEM((2, cpp, rows, D_HEAD), F32),
                            pltpu.SemaphoreType.DMA((2, 2, cpp)),
                            pltpu.VMEM((1, D_HEAD), F32),
                            pltpu.VMEM((h_b, 8, D_HEAD), F32)]),
        out_shape=jax.ShapeDtypeStruct((bd, h_b, D_HEAD), F32),
        compiler_params=_cparams(("arbitrary",), 32),
    )(page_table.reshape(-1), _page_rows(cache_k_b), _page_rows(cache_v_b),
      qb_s.reshape(bd, 1, wb), tri)


def _moba_s_kernel(pt_ref, ids_ref, ka_hbm, va_hbm, q_ref, kn_ref, vn_ref, o_ref, kbuf, vbuf, sem,
                   *, layer, n_pages, n_pg, ppb, page, h_a, n_steps, scale):
    step = pl.program_id(0) * h_a + pl.program_id(1)
    slot = jnp.bitwise_and(step, 1)

    def copies(st, sl):
        hh = lax.rem(st, h_a)
        out = []
        for j in range(n_pg):
            kk, half = divmod(j, ppb)
            pg = pt_ref[lax.div(st, h_a) * n_pages + ids_ref[st * MOBA_TOPK + kk] * ppb + half]
            out.append(pltpu.make_async_copy(ka_hbm.at[layer, pg, :, hh, :], kbuf.at[sl, j],
                                             sem.at[sl, 0, j]))
            out.append(pltpu.make_async_copy(va_hbm.at[layer, pg, :, hh, :], vbuf.at[sl, j],
                                             sem.at[sl, 1, j]))
        return out

    @pl.when(step == 0)
    def _():
        for cp in copies(step, slot):
            cp.start()

    @pl.when(step + 1 < n_steps)
    def _():
        for cp in copies(step + 1, 1 - slot):
            cp.start()

    for cp in copies(step, slot):
        cp.wait()

    q = q_ref[...]
    q8 = jnp.broadcast_to(q, (8, D_HEAD)).astype(BF16)
    s_self = jnp.sum(q * kn_ref[...], axis=1, keepdims=True) * scale
    scores = [(_nt_dot(q8, kbuf[slot, j].astype(BF16)) * scale)[0:1, :] for j in range(n_pg)]
    m = s_self
    for sc in scores:
        m = jnp.maximum(m, jnp.max(sc, axis=1, keepdims=True))
    p_self = jnp.exp(s_self - m)
    l = p_self
    acc = p_self * vn_ref[...]
    for j, sc in enumerate(scores):
        p = jnp.exp(sc - m)
        l = l + jnp.sum(p, axis=1, keepdims=True)
        p8 = jnp.broadcast_to(p, (8, p.shape[1])).astype(BF16)
        acc = acc + _dot(p8, vbuf[slot, j].astype(BF16))[0:1, :]
    o_ref[...] = acc / l


def _moba_sample(page_table, ids, cache_k_a, cache_v_a, qa_s, ka_s, va_s, layer):
    bd, n_pages = page_table.shape
    page, h_a = cache_k_a.shape[2], cache_k_a.shape[3]
    wa = h_a * D_HEAD
    ppb = MOBA_BLOCK // page
    n_pg = MOBA_TOPK * ppb
    ids_flat = ids[:, :, :MOBA_TOPK].reshape(-1)
    vec = pl.BlockSpec((None, None, 1, D_HEAD), lambda b, h, pt, idf: (b, h, 0, 0))
    hbm = pl.BlockSpec(memory_space=pl.ANY)
    kern = functools.partial(_moba_s_kernel, layer=layer, n_pages=n_pages, n_pg=n_pg, ppb=ppb,
                             page=page, h_a=h_a, n_steps=bd * h_a, scale=D_HEAD ** -0.5)
    out = pl.pallas_call(
        kern,
        grid_spec=pltpu.PrefetchScalarGridSpec(
            num_scalar_prefetch=2,
            grid=(bd, h_a),
            in_specs=[hbm, hbm, vec, vec, vec],
            out_specs=vec,
            scratch_shapes=[pltpu.VMEM((2, n_pg, page, D_HEAD), F32),
                            pltpu.VMEM((2, n_pg, page, D_HEAD), F32),
                            pltpu.SemaphoreType.DMA((2, 2, n_pg))]),
        out_shape=jax.ShapeDtypeStruct((bd, h_a, 1, D_HEAD), F32),
        compiler_params=_cparams(("arbitrary", "arbitrary"), 32),
    )(page_table.reshape(-1), ids_flat, cache_k_a, cache_v_a,
      qa_s.reshape(bd, h_a, 1, D_HEAD), ka_s.reshape(bd, h_a, 1, D_HEAD),
      va_s.reshape(bd, h_a, 1, D_HEAD))
    return out.reshape(bd, wa)


def _post_s_kernel(oa_ref, ga_ref, ob_ref, gb_ref, u_ref, gc_ref, st_ref, wp_ref, sc_ref,
                   wa_ref, wb_ref, wc_ref, x_ref, g_ref, b_ref,
                   of_ref, obf_ref, ns_ref, *, alpha, pos):
    cg = D_HEAD
    u = u_ref[...]
    outs = []
    for gi, w in enumerate(POOL_WINDOWS):
        sl = slice(gi * cg, (gi + 1) * cg)
        win = u[:, sl]
        for d in range(1, w):
            win = win + st_ref[POOL_BUF - d][:, sl]
        r = win / float(min(pos + 1, w)) - u[:, sl]
        outs.append(_dot(r.astype(BF16), wp_ref[gi].astype(BF16)))
    oc = jnp.concatenate(outs, axis=1) * sc_ref[...]
    ma = (oa_ref[...] * _silu(ga_ref[...])).astype(BF16)
    mb = (ob_ref[...] * _silu(gb_ref[...])).astype(BF16)
    mc = (oc * _silu(gc_ref[...])).astype(BF16)
    y = (alpha * x_ref[...]
         + _dot(ma, wa_ref[...])
         + _dot(mb, wb_ref[...])
         + _dot(mc, wc_ref[...]))
    out = _layer_norm(y, g_ref[...], b_ref[...])
    of_ref[...] = out
    obf_ref[...] = out.astype(BF16)
    for d in range(POOL_BUF - 1):
        ns_ref[d] = st_ref[d + 1]
    ns_ref[POOL_BUF - 1] = u


def _post_sample(oa, ga, ob, gb, uc, gc, state_t, w_pool, pool_scale, w_out_bf, xs,
                 ln_g, ln_b, layer, alpha, pos):
    r, d = xs.shape
    wa, wb, wc = oa.shape[1], ob.shape[1], uc.shape[1]
    kern = functools.partial(_post_s_kernel, alpha=alpha, pos=pos)
    full = lambda a: pl.BlockSpec(a.shape, lambda i: (0,) * a.ndim)
    vec = pl.BlockSpec((None, 1, d), lambda i: (layer, 0, 0))
    return pl.pallas_call(
        kern,
        grid=(1,),
        in_specs=[full(oa), full(ga), full(ob), full(gb), full(uc), full(gc),
                  pl.BlockSpec((None,) + state_t.shape[1:], lambda i: (layer, 0, 0, 0)),
                  pl.BlockSpec((None,) + w_pool.shape[1:], lambda i: (layer, 0, 0, 0)),
                  pl.BlockSpec((None, 1, wc), lambda i: (layer, 0, 0)),
                  pl.BlockSpec((None, wa, d), lambda i: (layer, 0, 0)),
                  pl.BlockSpec((None, wb, d), lambda i: (layer, wa // wb, 0)),
                  pl.BlockSpec((None, wc, d), lambda i: (layer, wa // wb + 1, 0)),
                  full(xs), vec, vec],
        out_specs=[pl.BlockSpec((r, d), lambda i: (0, 0)), pl.BlockSpec((r, d), lambda i: (0, 0)),
                   pl.BlockSpec((POOL_BUF, r, wc), lambda i: (0, 0, 0))],
        out_shape=[jax.ShapeDtypeStruct((r, d), F32), jax.ShapeDtypeStruct((r, d), BF16),
                   jax.ShapeDtypeStruct((POOL_BUF, r, wc), F32)],
        compiler_params=_cparams(("arbitrary",), 48),
    )(oa, ga, ob, gb, uc, gc, state_t, w_pool,
      pool_scale.reshape(pool_scale.shape[0], 1, wc), w_out_bf, w_out_bf, w_out_bf, xs,
      ln_g.reshape(ln_g.shape[0], 1, d), ln_b.reshape(ln_b.shape[0], 1, d))


def _rope_tables(pos):
    half = D_HEAD // 2
    inv = 1.0 / (ROPE_THETA ** (jnp.arange(half, dtype=F32) * (2.0 / D_HEAD)))
    ang = pos.astype(F32)[:, None] * inv[None, :]
    cos, sin = jnp.cos(ang), jnp.sin(ang)
    return jnp.concatenate([cos, cos], axis=1), jnp.concatenate([-sin, sin], axis=1)


def kernel(x_prompt, x_sample, cache_k_a, cache_v_a, cache_k_b, cache_v_b, state_pool,
           page_table, w_in, w_pool, pool_scale, w_out, ln_g, ln_b):
    b, t, d = x_prompt.shape
    bd, ts, _ = x_sample.shape
    assert ts == 1
    depth = w_in.shape[0]
    h_a, h_b = cache_k_a.shape[3], cache_k_b.shape[3]
    wa, wb = h_a * D_HEAD, h_b * D_HEAD
    wc = state_pool.shape[3]
    assert w_in.shape[2] == 4 * wa + 4 * wb + 2 * wc
    n_pages = page_table.shape[1]
    past_len = n_pages * cache_k_a.shape[2]
    alpha = (2 * depth) ** 0.25
    m = b * t

    w_in_bf = w_in
    w_out_bf = w_out.astype(BF16)
    cos_p, sin_p = _rope_tables(jnp.arange(t, dtype=jnp.int32))
    cos_s, sin_s = _rope_tables(jnp.full((bd,), past_len, jnp.int32))
    state_t = state_pool.transpose(0, 2, 1, 3)

    c_qa, c_ka, c_va, c_ga = 0, wa, 2 * wa, 3 * wa
    c_qb = 4 * wa
    c_kb, c_vb, c_gb = c_qb + wb, c_qb + 2 * wb, c_qb + 3 * wb
    c_uc = c_qb + 4 * wb
    c_gc = c_uc + wc

    xp = x_prompt.reshape(m, d)
    xp_bf = xp
    xs = x_sample.reshape(bd, d)
    xs_bf = xs.astype(BF16)

    pka, pva, pkb, pvb, ppool = [], [], [], [], []
    ska, sva, skb, svb, spool = [], [], [], [], []
    for l in range(depth):
        proj = functools.partial(_proj, xp_bf, w_in_bf, l, seq_len=t)
        qa, ka, kmean = proj(c_qa, [(wa, F32, False), (wa, F32, True)], cos_t=cos_p, sin_t=sin_p)
        va, ga = proj(c_va, [(wa, F32, False)] * 2)
        qb_bf, kb, vb, gb = proj(c_qb, [(wb, BF16, False)] + [(wb, F32, False)] * 3)
        uc, gc = proj(c_uc, [(wc, F32, False)] * 2)

        r3 = lambda a: a.reshape(b, t, a.shape[-1])
        mix_a = _moba_prompt(r3(qa), r3(ka), r3(va), kmean.reshape(b, t // MOBA_BLOCK, wa), r3(ga))
        mix_b = _sb_prompt(r3(qb_bf), r3(kb), r3(vb), r3(gb))
        mix_c = _pool_prompt(r3(uc), r3(gc), w_pool, pool_scale, l)
        xp, xp_bf = _merge(mix_a.reshape(m, wa), mix_b.reshape(m, wb), mix_c.reshape(m, wc),
                           w_out_bf, xp, ln_g, ln_b, l, alpha)
        pka.append(ka.reshape(b, t, h_a, D_HEAD))
        pva.append(va.reshape(b, t, h_a, D_HEAD))
        pkb.append(kb.reshape(b, t, h_b, D_HEAD))
        pvb.append(vb.reshape(b, t, h_b, D_HEAD))
        ppool.append(r3(uc)[:, t - POOL_BUF:, :])

        hs = _proj_sample(xs_bf, w_in_bf, l, cos_s, sin_s, 2 * wa)
        qa_s, ka_s = hs[:, c_qa:c_qa + wa], hs[:, c_ka:c_ka + wa]
        va_s, ga_s = hs[:, c_va:c_va + wa], hs[:, c_ga:c_ga + wa]
        qb_s, kb_s = hs[:, c_qb:c_qb + wb], hs[:, c_kb:c_kb + wb]
        vb_s, gb_s = hs[:, c_vb:c_vb + wb], hs[:, c_gb:c_gb + wb]
        uc_s, gc_s = hs[:, c_uc:c_uc + wc], hs[:, c_gc:c_gc + wc]
        ids = _gate_sample(page_table, cache_k_a, qa_s, l)
        ob_s = _sb_sample(page_table, cache_k_b, cache_v_b, qb_s, l)
        oa_s = _moba_sample(page_table, ids, cache_k_a, cache_v_a, qa_s, ka_s, va_s, l)
        xs, xs_bf, new_state = _post_sample(
            oa_s, ga_s, ob_s.reshape(bd, wb), gb_s, uc_s, gc_s, state_t, w_pool, pool_scale,
            w_out_bf, xs, ln_g, ln_b, l, alpha, past_len)
        ska.append(ka_s.reshape(bd, 1, h_a, D_HEAD))
        sva.append(va_s.reshape(bd, 1, h_a, D_HEAD))
        skb.append(kb_s.reshape(bd, 1, h_b, D_HEAD))
        svb.append(vb_s.reshape(bd, 1, h_b, D_HEAD))
        spool.append(new_state.transpose(1, 0, 2))

    return (xp.reshape(b, t, d), xs.reshape(bd, 1, d),
            jnp.stack(pka), jnp.stack(pva), jnp.stack(pkb), jnp.stack(pvb), jnp.stack(ppool),
            jnp.stack(ska), jnp.stack(sva), jnp.stack(skb), jnp.stack(svb), jnp.stack(spool))
```
